```python
import math
import jax, jax.numpy as jnp
from jax import lax
import numpy as np

D_MODEL = 2048
BATCH = 32
SEQ = 256
DEPTH = 4
DEC_BATCH = 8
DEC_SEQ = 4096
PAST_LEN = 512

GRID_W = 64
N_MIXERS = 2
N_A_LAYERS = (DEPTH + 1) // 2
N_B_LAYERS = DEPTH // 2
BLOCK = 128
A_HEADS = 16
A_KV_HEADS = 4
A_GROUPS = A_HEADS // A_KV_HEADS
A_HEAD_DIM = D_MODEL // A_HEADS
A_WINDOW = 128
B_HEADS = 8
B_QK_DIM = D_MODEL // (2 * B_HEADS)
B_V_DIM = 2 * B_QK_DIM
D_FF = 5632
CONV_WIDTH = 3
ROPE_BASE = 10000.0
EPS = 1e-6
NEG_INF = -1e30

kernel_name = 'hybrid_diffusion_window_sink_diff_attn_convglu'


def _rmsnorm(x, g):
    xf = x.astype(jnp.float32)
    y = xf * lax.rsqrt(jnp.mean(xf * xf, axis=-1, keepdims=True) + EPS)
    return (y * g.astype(jnp.float32)).astype(x.dtype)


def _modulation(cond, w_ada, b_ada):
    m = jax.nn.silu(cond) @ w_ada + b_ada
    return [t[:, None, :] for t in jnp.split(m, 6, axis=-1)]


def _adaln(x, g, shift, scale):
    return _rmsnorm(x, g) * (1 + scale) + shift


def _axial_angles(n, dim):
    rows = n // GRID_W
    row = jnp.repeat(jnp.arange(rows, dtype=jnp.float32), GRID_W)
    col = jnp.tile(jnp.arange(GRID_W, dtype=jnp.float32), rows)
    half = dim // 2
    inv = ROPE_BASE ** (-jnp.arange(0, half, 2, dtype=jnp.float32) / half)
    ang = jnp.concatenate([row[:, None] * inv, col[:, None] * inv], axis=-1)
    return jnp.cos(ang), jnp.sin(ang)


def _rotate(x, cos, sin):
    x1, x2 = jnp.split(x, 2, axis=-1)
    c = cos[None, :, None, :]
    s = sin[None, :, None, :]
    return jnp.concatenate([x1 * c - x2 * s, x2 * c + x1 * s], axis=-1)


def _axial_rope(x, cos, sin):
    half = x.shape[-1] // 2
    quarter = half // 2
    xf = x.astype(jnp.float32)
    out = jnp.concatenate([
        _rotate(xf[..., :half], cos[:, :quarter], sin[:, :quarter]),
        _rotate(xf[..., half:], cos[:, quarter:], sin[:, quarter:])], axis=-1)
    return out.astype(x.dtype)


def _to_blocks(q):
    b, n = q.shape[:2]
    return jnp.moveaxis(q.reshape((b, n // BLOCK, BLOCK) + q.shape[2:]), 1, 0)


def _from_blocks(o):
    nb, b, t, f = o.shape
    return jnp.moveaxis(o, 0, 1).reshape(b, nb * t, f)


def _a_project(h, w_qkv, gq, gk):
    b, n, _ = h.shape
    q, k, v = jnp.split(h @ w_qkv, [A_HEADS * A_HEAD_DIM, (A_HEADS + A_KV_HEADS) * A_HEAD_DIM], axis=-1)
    q = _rmsnorm(q.reshape(b, n, A_HEADS, A_HEAD_DIM), gq)
    k = _rmsnorm(k.reshape(b, n, A_KV_HEADS, A_HEAD_DIM), gk)
    v = v.reshape(b, n, A_KV_HEADS, A_HEAD_DIM)
    return q, k, v


def _gqa_sink_attend(q, k, v, sink, mask):
    b, tq = q.shape[:2]
    s = jnp.einsum('bqkgd,bskd->bkgqs', q.astype(jnp.float32), k.astype(jnp.float32)) * (A_HEAD_DIM ** -0.5)
    if mask is not None:
        s = jnp.where(mask, s, NEG_INF)
    sk = sink.astype(jnp.float32).reshape(1, A_KV_HEADS, A_GROUPS, 1, 1)
    m = jnp.maximum(jnp.max(s, axis=-1, keepdims=True), sk)
    p = jnp.exp(s - m)
    p = p / (jnp.sum(p, axis=-1, keepdims=True) + jnp.exp(sk - m))
    o = jnp.einsum('bkgqs,bskd->bqkgd', p, v.astype(jnp.float32))
    return o.reshape(b, tq, A_HEADS * A_HEAD_DIM).astype(v.dtype)


def _a_context_attention(q, k, v, sink):
    b, n = q.shape[:2]
    qb = _to_blocks(q.reshape(b, n, A_KV_HEADS, A_GROUPS, A_HEAD_DIM))
    o = lax.map(lambda q_blk: _gqa_sink_attend(q_blk, k, v, sink, None), qb)
    return _from_blocks(o)


def _a_latent_attention(q, k, v, k_ctx, v_ctx, sink):
    b, n = q.shape[:2]
    nb = n // BLOCK
    qb = _to_blocks(q.reshape(b, n, A_KV_HEADS, A_GROUPS, A_HEAD_DIM))
    pad = ((0, 0), (A_WINDOW, A_WINDOW), (0, 0), (0, 0))
    k_pad = jnp.pad(k, pad)
    v_pad = jnp.pad(v, pad)
    span = BLOCK + 2 * A_WINDOW
    q_off = jnp.arange(BLOCK)
    k_off = jnp.arange(span) - A_WINDOW
    ctx_mask = jnp.ones((BLOCK, k_ctx.shape[1]), dtype=bool)

    def one_block(args):
        j, q_blk = args
        start = j * BLOCK
        k_win = lax.dynamic_slice_in_dim(k_pad, start, span, axis=1)
        v_win = lax.dynamic_slice_in_dim(v_pad, start, span, axis=1)
        q_pos = start + q_off
        k_pos = start + k_off
        win_mask = ((jnp.abs(q_pos[:, None] - k_pos[None, :]) <= A_WINDOW)
                    & (k_pos >= 0)[None, :] & (k_pos < n)[None, :])
        mask = jnp.concatenate([win_mask, ctx_mask], axis=1)
        k_all = jnp.concatenate([k_win, k_ctx.astype(k_win.dtype)], axis=1)
        v_all = jnp.concatenate([v_win, v_ctx.astype(v_win.dtype)], axis=1)
        return _gqa_sink_attend(q_blk, k_all, v_all, sink, mask)

    o = lax.map(one_block, (jnp.arange(nb), qb))
    return _from_blocks(o)


def _b_project(h, w_qkv, gq, gk):
    b, n, _ = h.shape
    q, k, v = jnp.split(h @ w_qkv, 3, axis=-1)
    q = _rmsnorm(q.reshape(b, n, B_HEADS, 2, B_QK_DIM), gq)
    k = _rmsnorm(k.reshape(b, n, B_HEADS, 2, B_QK_DIM), gk)
    v = v.reshape(b, n, B_HEADS, B_V_DIM)
    return q, k, v


def _b_rope(x, cos, sin):
    b, n = x.shape[:2]
    return _axial_rope(x.reshape(b, n, B_HEADS * 2, B_QK_DIM), cos, sin).reshape(x.shape)


def _diff_attend(q, k, v, lam, subln_g, lambda_init):
    b, tq = q.shape[:2]
    s = jnp.einsum('bqhcd,bshcd->bhcqs', q.astype(jnp.float32), k.astype(jnp.float32)) * (B_QK_DIM ** -0.5)
    a = jax.nn.softmax(s, axis=-1)
    attn = a[:, :, 0] - lam * a[:, :, 1]
    o = jnp.einsum('bhqs,bshd->bqhd', attn, v.astype(jnp.float32))
    o = _rmsnorm(o, subln_g) * (1.0 - lambda_init)
    return o.reshape(b, tq, B_HEADS * B_V_DIM).astype(v.dtype)


def _b_attention(q, k_all, v_all, lam, subln_g, lambda_init):
    o = lax.map(lambda q_blk: _diff_attend(q_blk, k_all, v_all, lam, subln_g, lambda_init), _to_blocks(q))
    return _from_blocks(o)


def _conv_glu(h, w_up, conv_w, conv_b, w_down):
    g, v = jnp.split(h @ w_up, 2, axis=-1)
    n = g.shape[1]
    gp = jnp.pad(g, ((0, 0), (1, 1), (0, 0)))
    gc = gp[:, :n] * conv_w[0] + gp[:, 1:n + 1] * conv_w[1] + gp[:, 2:] * conv_w[2] + conv_b
    return (jax.nn.silu(gc) * v) @ w_down


def setup_inputs(seed: int = 0) -> dict:
    key = jax.random.key(seed)
    ks = jax.random.split(key, 32)
    f32 = jnp.float32

    def nrm(k, shape, scale):
        return jax.random.normal(k, shape, f32) * scale

    qkv_a = (A_HEADS + 2 * A_KV_HEADS) * A_HEAD_DIM
    return {
        'x_prompt': nrm(ks[0], (BATCH, SEQ, D_MODEL), 1.0),
        'x_sample': nrm(ks[1], (DEC_BATCH, DEC_SEQ, D_MODEL), 1.0),
        'cache_a_k': nrm(ks[2], (DEC_BATCH, N_A_LAYERS, PAST_LEN, A_KV_HEADS, A_HEAD_DIM), 1.0),
        'cache_a_v': nrm(ks[3], (DEC_BATCH, N_A_LAYERS, PAST_LEN, A_KV_HEADS, A_HEAD_DIM), 1.0),
        'cache_b_k': nrm(ks[4], (DEC_BATCH, N_B_LAYERS, PAST_LEN, B_HEADS, 2, B_QK_DIM), 1.0),
        'cache_b_v': nrm(ks[5], (DEC_BATCH, N_B_LAYERS, PAST_LEN, B_HEADS, B_V_DIM), 1.0),
        'c': nrm(ks[6], (DEC_BATCH, D_MODEL), 1.0),
        'c_ctx': nrm(ks[7], (D_MODEL,), 1.0),
        'ada_w': nrm(ks[8], (DEPTH, D_MODEL, 6 * D_MODEL), 0.5 * D_MODEL ** -0.5),
        'ada_b': nrm(ks[9], (DEPTH, 6 * D_MODEL), 0.02),
        'norm1_g': 1.0 + nrm(ks[10], (DEPTH, D_MODEL), 0.02),
        'norm2_g': 1.0 + nrm(ks[11], (DEPTH, D_MODEL), 0.02),
        'a_w_qkv': nrm(ks[12], (N_A_LAYERS, D_MODEL, qkv_a), D_MODEL ** -0.5),
        'a_q_norm': 1.0 + nrm(ks[13], (N_A_LAYERS, A_HEAD_DIM), 0.02),
        'a_k_norm': 1.0 + nrm(ks[14], (N_A_LAYERS, A_HEAD_DIM), 0.02),
        'a_sink': nrm(ks[15], (N_A_LAYERS, A_HEADS), 0.5),
        'a_w_o': nrm(ks[16], (N_A_LAYERS, A_HEADS * A_HEAD_DIM, D_MODEL), (A_HEADS * A_HEAD_DIM) ** -0.5),
        'b_w_qkv': nrm(ks[17], (N_B_LAYERS, D_MODEL, 3 * D_MODEL), D_MODEL ** -0.5),
        'b_q_norm': 1.0 + nrm(ks[18], (N_B_LAYERS, B_QK_DIM), 0.02),
        'b_k_norm': 1.0 + nrm(ks[19], (N_B_LAYERS, B_QK_DIM), 0.02),
        'b_lambda_q1': nrm(ks[20], (N_B_LAYERS, B_QK_DIM), 0.1),
        'b_lambda_k1': nrm(ks[21], (N_B_LAYERS, B_QK_DIM), 0.1),
        'b_lambda_q2': nrm(ks[22], (N_B_LAYERS, B_QK_DIM), 0.1),
        'b_lambda_k2': nrm(ks[23], (N_B_LAYERS, B_QK_DIM), 0.1),
        'b_subln': 1.0 + nrm(ks[24], (N_B_LAYERS, B_V_DIM), 0.02),
        'b_w_o': nrm(ks[25], (N_B_LAYERS, B_HEADS * B_V_DIM, D_MODEL), (B_HEADS * B_V_DIM) ** -0.5),
        'ffn_w_up': nrm(ks[26], (DEPTH, D_MODEL, 2 * D_FF), D_MODEL ** -0.5),
        'ffn_conv_w': nrm(ks[27], (DEPTH, CONV_WIDTH, D_FF), CONV_WIDTH ** -0.5),
        'ffn_conv_b': nrm(ks[28], (DEPTH, D_FF), 0.02),
        'ffn_w_down': nrm(ks[29], (DEPTH, D_FF, D_MODEL), D_FF ** -0.5),
    }


def reference(x_prompt, x_sample, cache_a_k, cache_a_v, cache_b_k, cache_b_v, c, c_ctx,
              ada_w, ada_b, norm1_g, norm2_g,
              a_w_qkv, a_q_norm, a_k_norm, a_sink, a_w_o,
              b_w_qkv, b_q_norm, b_k_norm, b_lambda_q1, b_lambda_k1, b_lambda_q2, b_lambda_k2, b_subln, b_w_o,
              ffn_w_up, ffn_conv_w, ffn_conv_b, ffn_w_down):
    n_lat = x_sample.shape[1]
    cos_a, sin_a = _axial_angles(n_lat, A_HEAD_DIM)
    cos_b, sin_b = _axial_angles(n_lat, B_QK_DIM)
    xp = x_prompt
    xs = x_sample
    a_k_list, a_v_list, b_k_list, b_v_list = [], [], [], []

    for i in range(DEPTH):
        sh1_p, sc1_p, g1_p, sh2_p, sc2_p, g2_p = _modulation(c_ctx[None, :], ada_w[i], ada_b[i])
        sh1_s, sc1_s, g1_s, sh2_s, sc2_s, g2_s = _modulation(c, ada_w[i], ada_b[i])
        hp = _adaln(xp, norm1_g[i], sh1_p, sc1_p)
        hs = _adaln(xs, norm1_g[i], sh1_s, sc1_s)
        j = i // N_MIXERS
        if i % N_MIXERS == 0:
            qp, kp, vp = _a_project(hp, a_w_qkv[j], a_q_norm[j], a_k_norm[j])
            op = _a_context_attention(qp, kp, vp, a_sink[j]) @ a_w_o[j]
            qs, ks_, vs = _a_project(hs, a_w_qkv[j], a_q_norm[j], a_k_norm[j])
            qs = _axial_rope(qs, cos_a, sin_a)
            ks_ = _axial_rope(ks_, cos_a, sin_a)
            os_ = _a_latent_attention(qs, ks_, vs, cache_a_k[:, j], cache_a_v[:, j], a_sink[j]) @ a_w_o[j]
            a_k_list.append(kp)
            a_v_list.append(vp)
        else:
            lambda_init = 0.8 - 0.6 * math.exp(-0.3 * i)
            lam = (jnp.exp(jnp.sum(b_lambda_q1[j].astype(jnp.float32) * b_lambda_k1[j].astype(jnp.float32)))
                   - jnp.exp(jnp.sum(b_lambda_q2[j].astype(jnp.float32) * b_lambda_k2[j].astype(jnp.float32)))
                   + lambda_init)
            qp, kp, vp = _b_project(hp, b_w_qkv[j], b_q_norm[j], b_k_norm[j])
            op = _b_attention(qp, kp, vp, lam, b_subln[j], lambda_init) @ b_w_o[j]
            qs, ks_, vs = _b_project(hs, b_w_qkv[j], b_q_norm[j], b_k_norm[j])
            qs = _b_rope(qs, cos_b, sin_b)
            ks_ = _b_rope(ks_, cos_b, sin_b)
            k_all = jnp.concatenate([ks_, cache_b_k[:, j].astype(ks_.dtype)], axis=1)
            v_all = jnp.concatenate([vs, cache_b_v[:, j].astype(vs.dtype)], axis=1)
            os_ = _b_attention(qs, k_all, v_all, lam, b_subln[j], lambda_init) @ b_w_o[j]
            b_k_list.append(kp)
            b_v_list.append(vp)
        xp = xp + g1_p * op
        xs = xs + g1_s * os_
        hp = _adaln(xp, norm2_g[i], sh2_p, sc2_p)
        hs = _adaln(xs, norm2_g[i], sh2_s, sc2_s)
        xp = xp + g2_p * _conv_glu(hp, ffn_w_up[i], ffn_conv_w[i], ffn_conv_b[i], ffn_w_down[i])
        xs = xs + g2_s * _conv_glu(hs, ffn_w_up[i], ffn_conv_w[i], ffn_conv_b[i], ffn_w_down[i])

    state_a_k = jnp.stack(a_k_list, axis=1)
    state_a_v = jnp.stack(a_v_list, axis=1)
    state_b_k = jnp.stack(b_k_list, axis=1)
    state_b_v = jnp.stack(b_v_list, axis=1)
    return (xp, xs, state_a_k, state_a_v, state_b_k, state_b_v)
```

```python
import functools
import math

import jax
import jax.numpy as jnp
from jax import lax
from jax.experimental import pallas as pl
from jax.experimental.pallas import tpu as pltpu

GRID_W = 64
A_WINDOW = 128
ROPE_BASE = 10000.0
EPS = 1e-6
NEG_INF = -1e30
LANES = 128
VMEM_LIMIT = 56 * 1024 * 1024

BF16 = jnp.bfloat16
F32 = jnp.float32


def _params(*sem):
    return pltpu.CompilerParams(dimension_semantics=sem, vmem_limit_bytes=VMEM_LIMIT)


def _tile(size, target, *also):
    t = min(target, size)
    while any(s % t for s in (size,) + also):
        t //= 2
    return t


def _rms(x):
    return x * lax.rsqrt(jnp.mean(x * x, axis=-1, keepdims=True) + EPS)


def _adaln(x, g, shift, scale):
    return _rms(x) * g * (1.0 + scale) + shift


def _mod_index(per_batch, tiles_per_seq):
    if per_batch:
        return lambda i, j: (i // tiles_per_seq, 0, 0)
    return lambda i, j: (0, 0, 0)


def _modulation_kernel(c_ref, w_ref, b_ref, o_ref):
    c = c_ref[...]
    s = (c / (1.0 + jnp.exp(-c))).astype(BF16)
    o_ref[0] = jnp.dot(s, w_ref[0].astype(BF16), preferred_element_type=F32) + b_ref[0]


def _modulation(cond, ada_w, ada_b, tn=1024):
    depth, d, n = ada_w.shape
    r = cond.shape[0]
    return pl.pallas_call(
        _modulation_kernel,
        grid=(depth, n // tn),
        in_specs=[pl.BlockSpec((r, d), lambda l, j: (0, 0)),
                  pl.BlockSpec((1, d, tn), lambda l, j: (l, 0, j)),
                  pl.BlockSpec((1, 1, tn), lambda l, j: (l, 0, j))],
        out_specs=pl.BlockSpec((1, r, tn), lambda l, j: (l, 0, j)),
        out_shape=jax.ShapeDtypeStruct((depth, r, n), F32),
        compiler_params=_params("arbitrary", "arbitrary"),
        name="modulation",
    )(cond, ada_w, ada_b.reshape(depth, 1, n))


def _rope(y, cos, sin_signed):
    lane = lax.broadcasted_iota(jnp.int32, y.shape, 1)
    partner = jnp.where((lane & 63) < 32, pltpu.roll(y, LANES - 32, 1), pltpu.roll(y, 32, 1))
    return y * cos + partner * sin_signed


def _qkv_kernel(*refs, n_norm_tiles, rope):
    if rope:
        x_ref, g_ref, sh_ref, sc_ref, w_ref, hg_ref, cos_ref, sin_ref, o_ref, h_ref = refs
    else:
        x_ref, g_ref, sh_ref, sc_ref, w_ref, hg_ref, o_ref, h_ref = refs
    j = pl.program_id(1)

    @pl.when(j == 0)
    def _():
        h_ref[...] = _adaln(x_ref[...], g_ref[...], sh_ref[0], sc_ref[0]).astype(BF16)

    y = jnp.dot(h_ref[...], w_ref[...], preferred_element_type=F32)
    tn = y.shape[1]

    @pl.when(j < n_norm_tiles)
    def _():
        for c in range(tn // LANES):
            sl = slice(c * LANES, (c + 1) * LANES)
            yc = _rms(y[:, sl]) * hg_ref[:, sl]
            if rope:
                yc = _rope(yc, cos_ref[...], sin_ref[...])
            o_ref[:, sl] = yc.astype(o_ref.dtype)

    @pl.when(j >= n_norm_tiles)
    def _():
        o_ref[...] = y.astype(o_ref.dtype)


def _qkv(x, g, shift, scale, w, head_gain, n_norm_cols, seq_len, rope_tabs, out_dtype, tm, tn):
    m, d = x.shape
    n = w.shape[1]
    per_batch = shift.shape[0] > 1
    tn = _tile(n, tn, n_norm_cols)
    tm = _tile(m, tm, seq_len) if (per_batch or rope_tabs is not None) else _tile(m, tm)
    tiles_per_seq = max(seq_len // tm, 1)
    mod_idx = _mod_index(per_batch, tiles_per_seq)
    in_specs = [pl.BlockSpec((tm, d), lambda i, j: (i, 0)),
                pl.BlockSpec((1, d), lambda i, j: (0, 0)),
                pl.BlockSpec((1, 1, d), mod_idx),
                pl.BlockSpec((1, 1, d), mod_idx),
                pl.BlockSpec((d, tn), lambda i, j: (0, j)),
                pl.BlockSpec((1, tn), lambda i, j: (0, j))]
    args = [x, g, shift, scale, w, head_gain]
    if rope_tabs is not None:
        in_specs += [pl.BlockSpec((tm, LANES), lambda i, j: (i % tiles_per_seq, 0))] * 2
        args += list(rope_tabs)
    return pl.pallas_call(
        functools.partial(_qkv_kernel, n_norm_tiles=n_norm_cols // tn, rope=rope_tabs is not None),
        grid=(m // tm, n // tn),
        in_specs=in_specs,
        out_specs=pl.BlockSpec((tm, tn), lambda i, j: (i, j)),
        out_shape=jax.ShapeDtypeStruct((m, n), out_dtype),
        scratch_shapes=[pltpu.VMEM((tm, d), BF16)],
        compiler_params=_params("parallel", "arbitrary"),
        name="adaln_qkv",
    )(*args)


def _attn_a_kernel(*refs, groups, window, tq, n_blocks):
    if window:
        (sink_ref, q_ref, kp_ref, km_ref, kn_ref, kc_ref, vp_ref, vm_ref, vn_ref, vc_ref, o_ref) = refs
    else:
        sink_ref, q_ref, km_ref, vm_ref, o_ref = refs
    kvh = pl.program_id(2)
    if window:
        i = pl.program_id(1)
        k_all = jnp.concatenate([kp_ref[...], km_ref[...], kn_ref[...], kc_ref[0]], axis=0)
        v_all = jnp.concatenate([vp_ref[...], vm_ref[...], vn_ref[...], vc_ref[0]], axis=0)
        n_lat = tq + 2 * A_WINDOW
        qi = lax.broadcasted_iota(jnp.int32, (tq, k_all.shape[0]), 0)
        kj = lax.broadcasted_iota(jnp.int32, (tq, k_all.shape[0]), 1)
        in_seq = ((i > 0) | (kj >= A_WINDOW)) & ((i < n_blocks - 1) | (kj < tq + A_WINDOW))
        mask = (kj >= n_lat) | ((jnp.abs(kj - A_WINDOW - qi) <= A_WINDOW) & in_seq)
    else:
        k_all = km_ref[...].astype(BF16)
        v_all = vm_ref[...].astype(BF16)
    for g in range(groups):
        sl = slice(g * LANES, (g + 1) * LANES)
        q = q_ref[:, sl].astype(BF16)
        s = lax.dot_general(q, k_all, (((1,), (1,)), ((), ())), preferred_element_type=F32)
        if window:
            s = jnp.where(mask, s, NEG_INF)
        sink = sink_ref[0, kvh * groups + g]
        m = jnp.maximum(jnp.max(s, axis=-1, keepdims=True), sink)
        p = jnp.exp(s - m)
        denom = jnp.sum(p, axis=-1, keepdims=True) + jnp.exp(sink - m)
        o = jnp.dot(p.astype(BF16), v_all, preferred_element_type=F32)
        o_ref[:, sl] = (o / denom).astype(o_ref.dtype)


def _attn_a(qkv, sink, batch, seq_len, heads, kv_heads, ctx_k=None, ctx_v=None, tq=256):
    m = qkv.shape[0]
    groups = heads // kv_heads
    window = ctx_k is not None
    nb = seq_len // tq
    k_col, v_col = heads, heads + kv_heads
    r = tq // A_WINDOW
    rows128 = seq_len // A_WINDOW

    def main_rows(b, i):
        return b * nb + i

    def prev_rows(b, i):
        return b * rows128 + jnp.maximum(i * r - 1, 0)

    def next_rows(b, i):
        return b * rows128 + jnp.minimum((i + 1) * r, rows128 - 1)

    sink_spec = pl.BlockSpec(memory_space=pltpu.SMEM)
    q_spec = pl.BlockSpec((tq, groups * LANES), lambda b, i, k: (main_rows(b, i), k))

    def kv_specs(col):
        main = pl.BlockSpec((tq, LANES), lambda b, i, k: (main_rows(b, i), col + k))
        if not window:
            return [main]
        return [pl.BlockSpec((A_WINDOW, LANES), lambda b, i, k: (prev_rows(b, i), col + k)),
                main,
                pl.BlockSpec((A_WINDOW, LANES), lambda b, i, k: (next_rows(b, i), col + k)),
                pl.BlockSpec((1, ctx_k.shape[1], LANES), lambda b, i, k: (b, 0, k))]

    if window:
        args = [sink, qkv, qkv, qkv, qkv, ctx_k, qkv, qkv, qkv, ctx_v]
    else:
        args = [sink, qkv, qkv, qkv]
    return pl.pallas_call(
        functools.partial(_attn_a_kernel, groups=groups, window=window, tq=tq, n_blocks=nb),
        grid=(batch, nb, kv_heads),
        in_specs=[sink_spec, q_spec] + kv_specs(k_col) + kv_specs(v_col),
        out_specs=pl.BlockSpec((tq, groups * LANES), lambda b, i, k: (main_rows(b, i), k)),
        out_shape=jax.ShapeDtypeStruct((m, heads * LANES), BF16),
        compiler_params=_params("parallel", "arbitrary", "arbitrary"),
        name="attn_a_window" if window else "attn_a_ctx",
    )(*args)


def _attn_b_kernel(*refs, has_ctx, out_scale):
    if has_ctx:
        lam_ref, q_ref, k_ref, v_ref, kc_ref, vc_ref, sg_ref, o_ref = refs
        ks = [k_ref[...], kc_ref[0]]
        vs = [v_ref[...], vc_ref[0]]
    else:
        lam_ref, q_ref, k_ref, v_ref, sg_ref, o_ref = refs
        ks = [k_ref[...].astype(BF16)]
        vs = [v_ref[...].astype(BF16)]
    lam = lam_ref[0, 0]
    ps, ls = [], []
    for c in range(2):
        sl = slice(c * LANES, (c + 1) * LANES)
        q = q_ref[:, sl].astype(BF16)
        ss = [lax.dot_general(q, k[:, sl], (((1,), (1,)), ((), ())), preferred_element_type=F32) for k in ks]
        m = functools.reduce(jnp.maximum, [jnp.max(s, axis=-1, keepdims=True) for s in ss])
        pc = [jnp.exp(s - m) for s in ss]
        ps.append(pc)
        ls.append(functools.reduce(jnp.add, [jnp.sum(p, axis=-1, keepdims=True) for p in pc]))
    r1 = 1.0 / ls[0]
    r2 = lam / ls[1]
    o = None
    for p1, p2, v in zip(ps[0], ps[1], vs):
        a = (p1 * r1 - p2 * r2).astype(BF16)
        t = jnp.dot(a, v, preferred_element_type=F32)
        o = t if o is None else o + t
    o_ref[...] = (_rms(o) * sg_ref[...] * out_scale).astype(o_ref.dtype)


def _attn_b(qkv, lam, subln_g, out_scale, batch, seq_len, heads, ctx_k=None, ctx_v=None, tq=256):
    m = qkv.shape[0]
    hd = 2 * LANES
    has_ctx = ctx_k is not None
    nb = seq_len // tq
    in_specs = [pl.BlockSpec(memory_space=pltpu.SMEM),
                pl.BlockSpec((tq, hd), lambda b, h, i: (b * nb + i, h)),
                pl.BlockSpec((seq_len, hd), lambda b, h, i: (b, heads + h)),
                pl.BlockSpec((seq_len, hd), lambda b, h, i: (b, 2 * heads + h))]
    args = [lam, qkv, qkv, qkv]
    if has_ctx:
        in_specs += [pl.BlockSpec((1, ctx_k.shape[1], hd), lambda b, h, i: (b, 0, h))] * 2
        args += [ctx_k, ctx_v]
    in_specs.append(pl.BlockSpec((1, hd), lambda b, h, i: (0, 0)))
    args.append(subln_g)
    return pl.pallas_call(
        functools.partial(_attn_b_kernel, has_ctx=has_ctx, out_scale=out_scale),
        grid=(batch, heads, nb),
        in_specs=in_specs,
        out_specs=pl.BlockSpec((tq, hd), lambda b, h, i: (b * nb + i, h)),
        out_shape=jax.ShapeDtypeStruct((m, heads * hd), BF16),
        compiler_params=_params("parallel", "arbitrary", "arbitrary"),
        name="attn_b_latent" if has_ctx else "attn_b_ctx",
    )(*args)


def _proj_residual_kernel(a_ref, w_ref, x_ref, gate_ref, o_ref):
    y = jnp.dot(a_ref[...], w_ref[...], preferred_element_type=F32)
    o_ref[...] = x_ref[...] + gate_ref[0] * y


def _proj_residual(a, w, x, gate, seq_len, tm, tn):
    m, k = a.shape
    n = w.shape[1]
    per_batch = gate.shape[0] > 1
    tm = _tile(m, tm, seq_len) if per_batch else _tile(m, tm)
    tn = _tile(n, tn)
    mod_idx = _mod_index(per_batch, max(seq_len // tm, 1))
    gate_idx = lambda i, j: mod_idx(i, j)[:2] + (j,)
    return pl.pallas_call(
        _proj_residual_kernel,
        grid=(m // tm, n // tn),
        in_specs=[pl.BlockSpec((tm, k), lambda i, j: (i, 0)),
                  pl.BlockSpec((k, tn), lambda i, j: (0, j)),
                  pl.BlockSpec((tm, tn), lambda i, j: (i, j)),
                  pl.BlockSpec((1, 1, tn), gate_idx)],
        out_specs=pl.BlockSpec((tm, tn), lambda i, j: (i, j)),
        out_shape=jax.ShapeDtypeStruct((m, n), F32),
        compiler_params=_params("parallel", "arbitrary"),
        name="proj_residual",
    )(a, w, x, gate)


HALO = 16


def _ffn_kernel(xp_ref, x_ref, xn_ref, g_ref, sh_ref, sc_ref, gate_ref, wg_ref, wv_ref, cw_ref, cb_ref, wd_ref,
                o_ref, h_ref, acc_ref, *, seq_len, tm):
    i = pl.program_id(0)
    f = pl.program_id(1)

    @pl.when(f == 0)
    def _():
        g, sh, sc = g_ref[...], sh_ref[0], sc_ref[0]
        h_ref[0:HALO] = _adaln(xp_ref[...], g, sh, sc).astype(BF16)
        h_ref[HALO:HALO + tm] = _adaln(x_ref[...], g, sh, sc).astype(BF16)
        h_ref[HALO + tm:] = _adaln(xn_ref[...], g, sh, sc).astype(BF16)
        acc_ref[...] = jnp.zeros_like(acc_ref)

    gate_pre = jnp.dot(h_ref[...], wg_ref[...], preferred_element_type=F32)
    val = jnp.dot(h_ref[HALO:HALO + tm], wv_ref[...], preferred_element_type=F32)
    pos = (i * tm + lax.broadcasted_iota(jnp.int32, (tm, 1), 0)) & (seq_len - 1)
    up = jnp.where(pos != 0, gate_pre[HALO - 1:HALO - 1 + tm], 0.0)
    dn = jnp.where(pos != seq_len - 1, gate_pre[HALO + 1:HALO + 1 + tm], 0.0)
    gc = up * cw_ref[0:1] + gate_pre[HALO:HALO + tm] * cw_ref[1:2] + dn * cw_ref[2:3] + cb_ref[...]
    act = (gc / (1.0 + jnp.exp(-gc)) * val).astype(BF16)
    acc_ref[...] += jnp.dot(act, wd_ref[...], preferred_element_type=F32)

    @pl.when(f == pl.num_programs(1) - 1)
    def _():
        o_ref[...] = x_ref[...] + gate_ref[0] * acc_ref[...]


def _ffn(x, g, shift, scale, gate, w_up, conv_w, conv_b, w_down, seq_len, tm, tf):
    m, d = x.shape
    ff = w_down.shape[0]
    per_batch = shift.shape[0] > 1
    tm = _tile(m, tm, seq_len) if per_batch else _tile(m, tm)
    tf = _tile(ff, tf)
    assert (seq_len % tm == 0 or tm % seq_len == 0) and tm % HALO == 0
    assert seq_len & (seq_len - 1) == 0, "token position uses a power-of-two mask"
    mod_idx = _mod_index(per_batch, max(seq_len // tm, 1))
    hb = tm // HALO
    n_halo_blocks = m // HALO
    nf = ff // tf
    return pl.pallas_call(
        functools.partial(_ffn_kernel, seq_len=seq_len, tm=tm),
        grid=(m // tm, nf),
        in_specs=[pl.BlockSpec((HALO, d), lambda i, f: (jnp.maximum(i * hb - 1, 0), 0)),
                  pl.BlockSpec((tm, d), lambda i, f: (i, 0)),
                  pl.BlockSpec((HALO, d), lambda i, f: (jnp.minimum((i + 1) * hb, n_halo_blocks - 1), 0)),
                  pl.BlockSpec((1, d), lambda i, f: (0, 0)),
                  pl.BlockSpec((1, 1, d), mod_idx),
                  pl.BlockSpec((1, 1, d), mod_idx),
                  pl.BlockSpec((1, 1, d), mod_idx),
                  pl.BlockSpec((d, tf), lambda i, f: (0, f)),
                  pl.BlockSpec((d, tf), lambda i, f: (0, nf + f)),
                  pl.BlockSpec((3, tf), lambda i, f: (0, f)),
                  pl.BlockSpec((1, tf), lambda i, f: (0, f)),
                  pl.BlockSpec((tf, d), lambda i, f: (f, 0))],
        out_specs=pl.BlockSpec((tm, d), lambda i, f: (i, 0)),
        out_shape=jax.ShapeDtypeStruct((m, d), F32),
        scratch_shapes=[pltpu.VMEM((tm + 2 * HALO, d), BF16), pltpu.VMEM((tm, d), F32)],
        compiler_params=_params("parallel", "arbitrary"),
        name="conv_glu",
    )(x, x, x, g, shift, scale, gate, w_up, w_up, conv_w, conv_b, w_down)


def _rope_tables(n, dim):
    rows = n // GRID_W
    row = jnp.repeat(jnp.arange(rows, dtype=F32), GRID_W)
    col = jnp.tile(jnp.arange(GRID_W, dtype=F32), rows)
    half = dim // 2
    inv = ROPE_BASE ** (-jnp.arange(0, half, 2, dtype=F32) / half)
    ar, ac = row[:, None] * inv, col[:, None] * inv
    cos = jnp.concatenate([jnp.cos(ar), jnp.cos(ar), jnp.cos(ac), jnp.cos(ac)], axis=-1)
    sin = jnp.concatenate([-jnp.sin(ar), jnp.sin(ar), -jnp.sin(ac), jnp.sin(ac)], axis=-1)
    return cos, sin


def kernel(x_prompt, x_sample, cache_a_k, cache_a_v, cache_b_k, cache_b_v, c, c_ctx, ada_w, ada_b, norm1_g, norm2_g, a_w_qkv, a_q_norm, a_k_norm, a_sink, a_w_o, b_w_qkv, b_q_norm, b_k_norm, b_lambda_q1, b_lambda_k1, b_lambda_q2, b_lambda_k2, b_subln, b_w_o, ffn_w_up, ffn_conv_w, ffn_conv_b, ffn_w_down):
    batch, seq, d = x_prompt.shape
    dec_batch, dec_seq, _ = x_sample.shape
    depth = ada_w.shape[0]
    a_heads = a_sink.shape[1]
    a_kvh, a_hd = cache_a_k.shape[3], cache_a_k.shape[4]
    b_heads, b_qk = cache_b_k.shape[3], cache_b_k.shape[5]
    b_vd = cache_b_v.shape[4]
    past = cache_a_k.shape[2]
    assert a_hd == LANES and b_qk == LANES and b_vd == 2 * LANES

    cond = jnp.concatenate([c, c_ctx[None, :], jnp.zeros((16 - dec_batch - 1, d), F32)], axis=0)
    mod = _modulation(cond, ada_w, ada_b).reshape(depth, 16, 6, d)
    cos, sin = _rope_tables(dec_seq, LANES)

    xp = x_prompt.reshape(batch * seq, d)
    xs = x_sample.reshape(dec_batch * dec_seq, d)
    a_k_list, a_v_list, b_k_list, b_v_list = [], [], [], []

    for l in range(depth):
        mod_s = [mod[l, :dec_batch, t][:, None, :] for t in range(6)]
        mod_p = [mod[l, dec_batch:dec_batch + 1, t][:, None, :] for t in range(6)]
        n1 = norm1_g[l][None, :]
        n2 = norm2_g[l][None, :]
        j = l // 2
        if l % 2 == 0:
            w_qkv = a_w_qkv[j].astype(BF16)
            w_o = a_w_o[j].astype(BF16)
            qn, kn = a_heads * a_hd, a_kvh * a_hd
            head_gain = jnp.concatenate([jnp.tile(a_q_norm[j] * (a_hd ** -0.5), a_heads),
                                         jnp.tile(a_k_norm[j], a_kvh), jnp.ones((kn,), F32)])[None, :]
            sink = a_sink[j][None, :]
            qkv_p = _qkv(xp, n1, mod_p[0], mod_p[1], w_qkv, head_gain, qn + kn, seq, None, F32, tm=512, tn=512)
            op = _attn_a(qkv_p, sink, batch, seq, a_heads, a_kvh)
            qkv_s = _qkv(xs, n1, mod_s[0], mod_s[1], w_qkv, head_gain, qn + kn, dec_seq, (cos, sin), BF16,
                         tm=1024, tn=512)
            ctx_k = cache_a_k[:, j].reshape(dec_batch, past, kn).astype(BF16)
            ctx_v = cache_a_v[:, j].reshape(dec_batch, past, kn).astype(BF16)
            os_ = _attn_a(qkv_s, sink, dec_batch, dec_seq, a_heads, a_kvh, ctx_k, ctx_v)
            a_k_list.append(qkv_p[:, qn:qn + kn].reshape(batch, seq, a_kvh, a_hd))
            a_v_list.append(qkv_p[:, qn + kn:].reshape(batch, seq, a_kvh, a_hd))
        else:
            w_qkv = b_w_qkv[j].astype(BF16)
            w_o = b_w_o[j].astype(BF16)
            lambda_init = 0.8 - 0.6 * math.exp(-0.3 * l)
            lam = (jnp.exp(jnp.sum(b_lambda_q1[j] * b_lambda_k1[j])) - jnp.exp(jnp.sum(b_lambda_q2[j] * b_lambda_k2[j]))
                   + lambda_init).reshape(1, 1)
            qn = b_heads * 2 * b_qk
            head_gain = jnp.concatenate([jnp.tile(b_q_norm[j] * (b_qk ** -0.5), 2 * b_heads),
                                         jnp.tile(b_k_norm[j], 2 * b_heads), jnp.ones((qn,), F32)])[None, :]
            subln = b_subln[j][None, :]
            qkv_p = _qkv(xp, n1, mod_p[0], mod_p[1], w_qkv, head_gain, 2 * qn, seq, None, F32, tm=512, tn=512)
            op = _attn_b(qkv_p, lam, subln, 1.0 - lambda_init, batch, seq, b_heads)
            qkv_s = _qkv(xs, n1, mod_s[0], mod_s[1], w_qkv, head_gain, 2 * qn, dec_seq, (cos, sin), BF16,
                         tm=1024, tn=512)
            ctx_k = cache_b_k[:, j].reshape(dec_batch, past, qn).astype(BF16)
            ctx_v = cache_b_v[:, j].reshape(dec_batch, past, qn).astype(BF16)
            os_ = _attn_b(qkv_s, lam, subln, 1.0 - lambda_init, dec_batch, dec_seq, b_heads, ctx_k, ctx_v)
            b_k_list.append(qkv_p[:, qn:2 * qn].reshape(batch, seq, b_heads, 2, b_qk))
            b_v_list.append(qkv_p[:, 2 * qn:].reshape(batch, seq, b_heads, b_vd))
        xp = _proj_residual(op, w_o, xp, mod_p[2], seq, tm=1024, tn=1024)
        xs = _proj_residual(os_, w_o, xs, mod_s[2], dec_seq, tm=1024, tn=1024)
        w_up = ffn_w_up[l].astype(BF16)
        w_down = ffn_w_down[l].astype(BF16)
        cb = ffn_conv_b[l][None, :]
        xp = _ffn(xp, n2, mod_p[3], mod_p[4], mod_p[5], w_up, ffn_conv_w[l], cb, w_down, seq, tm=512, tf=512)
        xs = _ffn(xs, n2, mod_s[3], mod_s[4], mod_s[5], w_up, ffn_conv_w[l], cb, w_down, dec_seq, tm=512, tf=512)

    return (xp.reshape(batch, seq, d), xs.reshape(dec_batch, dec_seq, d),
            jnp.stack(a_k_list, axis=1), jnp.stack(a_v_list, axis=1),
            jnp.stack(b_k_list, axis=1), jnp.stack(b_v_list, axis=1))
```

```python
import functools
import math

import jax
import jax.numpy as jnp
from jax import lax
from jax.experimental import pallas as pl
from jax.experimental.pallas import tpu as pltpu

GRID_W = 64
A_WINDOW = 128
ROPE_BASE = 10000.0
EPS = 1e-6
NEG_INF = -1e30
LANES = 128
MXU_COLS = 256
LOG2E = math.log2(math.e)
VMEM_LIMIT = 56 * 1024 * 1024

BF16 = jnp.bfloat16
F32 = jnp.float32


def _params(*sem):
    return pltpu.CompilerParams(dimension_semantics=sem, vmem_limit_bytes=VMEM_LIMIT)


def _tile(size, target, *also):
    t = min(target, size)
    while any(s % t for s in (size,) + also):
        t //= 2
    return t


def _rms(x):
    return x * lax.rsqrt(jnp.mean(x * x, axis=-1, keepdims=True) + EPS)


def _adaln(x, g, shift, scale):
    return _rms(x) * g * (1.0 + scale) + shift


def _mod_index(per_batch, tiles_per_seq):
    if per_batch:
        return lambda i, j: (i // tiles_per_seq, 0, 0)
    return lambda i, j: (0, 0, 0)


def _modulation_kernel(c_ref, w_ref, b_ref, o_ref):
    c = c_ref[...]
    s = (c / (1.0 + jnp.exp(-c))).astype(BF16)
    o_ref[0] = jnp.dot(s, w_ref[0].astype(BF16), preferred_element_type=F32) + b_ref[0]


def _modulation(cond, ada_w, ada_b, tn=1024):
    depth, d, n = ada_w.shape
    r = cond.shape[0]
    return pl.pallas_call(
        _modulation_kernel,
        grid=(depth, n // tn),
        in_specs=[pl.BlockSpec((r, d), lambda l, j: (0, 0)),
                  pl.BlockSpec((1, d, tn), lambda l, j: (l, 0, j)),
                  pl.BlockSpec((1, 1, tn), lambda l, j: (l, 0, j))],
        out_specs=pl.BlockSpec((1, r, tn), lambda l, j: (l, 0, j)),
        out_shape=jax.ShapeDtypeStruct((depth, r, n), F32),
        compiler_params=_params("arbitrary", "arbitrary"),
        name="modulation",
    )(cond, ada_w, ada_b.reshape(depth, 1, n))


def _head_matrices():
    i = jnp.arange(MXU_COLS)
    mean = jnp.where(i[:, None] // LANES == i[None, :] // LANES, 1.0 / LANES, 0.0)
    partner = jnp.where((i & 63) < 32, i + 32, i - 32)
    perm = (i[:, None] == partner[None, :])
    return mean.astype(BF16), perm.astype(BF16)


def _qkv_kernel(*refs, n_norm_tiles, rope):
    if rope:
        x_ref, g_ref, sh_ref, sc_ref, w_ref, hg_ref, mean_ref, perm_ref, cos_ref, sin_ref, o_ref, h_ref = refs
    else:
        x_ref, g_ref, sh_ref, sc_ref, w_ref, hg_ref, mean_ref, o_ref, h_ref = refs
    j = pl.program_id(1)

    @pl.when(j == 0)
    def _():
        h_ref[...] = _adaln(x_ref[...], g_ref[...], sh_ref[0], sc_ref[0]).astype(BF16)

    tn = o_ref.shape[1]
    y_all = jnp.dot(h_ref[...], w_ref[...], preferred_element_type=F32)

    @pl.when(j < n_norm_tiles)
    def _():
        for c in range(tn // MXU_COLS):
            sl = slice(c * MXU_COLS, (c + 1) * MXU_COLS)
            y = y_all[:, sl]
            ms = jnp.dot((y * y).astype(BF16), mean_ref[...], preferred_element_type=F32)
            z = y * hg_ref[:, sl]
            if rope:
                partner = jnp.dot(z.astype(BF16), perm_ref[...], preferred_element_type=F32)
                z = z * cos_ref[...] + partner * sin_ref[...]
            o_ref[:, sl] = (z * lax.rsqrt(ms + EPS)).astype(o_ref.dtype)

    @pl.when(j >= n_norm_tiles)
    def _():
        o_ref[...] = y_all.astype(o_ref.dtype)


def _qkv(x, g, shift, scale, w, head_gain, n_norm_cols, seq_len, rope_tabs, out_dtype, tm, tn):
    m, d = x.shape
    n = w.shape[1]
    per_batch = shift.shape[0] > 1
    tn = _tile(n, tn, n_norm_cols)
    assert tn % MXU_COLS == 0
    tm = _tile(m, tm, seq_len) if (per_batch or rope_tabs is not None) else _tile(m, tm)
    tiles_per_seq = max(seq_len // tm, 1)
    mod_idx = _mod_index(per_batch, tiles_per_seq)
    mean_mat, perm_mat = _head_matrices()
    const_spec = pl.BlockSpec((MXU_COLS, MXU_COLS), lambda i, j: (0, 0))
    in_specs = [pl.BlockSpec((tm, d), lambda i, j: (i, 0)),
                pl.BlockSpec((1, d), lambda i, j: (0, 0)),
                pl.BlockSpec((1, 1, d), mod_idx),
                pl.BlockSpec((1, 1, d), mod_idx),
                pl.BlockSpec((d, tn), lambda i, j: (0, j)),
                pl.BlockSpec((1, tn), lambda i, j: (0, j)),
                const_spec]
    args = [x, g, shift, scale, w, head_gain, mean_mat]
    if rope_tabs is not None:
        in_specs += [const_spec] + [pl.BlockSpec((tm, MXU_COLS), lambda i, j: (i % tiles_per_seq, 0))] * 2
        args += [perm_mat] + list(rope_tabs)
    return pl.pallas_call(
        functools.partial(_qkv_kernel, n_norm_tiles=n_norm_cols // tn, rope=rope_tabs is not None),
        grid=(m // tm, n // tn),
        in_specs=in_specs,
        out_specs=pl.BlockSpec((tm, tn), lambda i, j: (i, j)),
        out_shape=jax.ShapeDtypeStruct((m, n), out_dtype),
        scratch_shapes=[pltpu.VMEM((tm, d), BF16)],
        compiler_params=_params("parallel", "arbitrary"),
        name="adaln_qkv",
    )(*args)


def _attn_a_kernel(*refs, groups, window, tq, n_blocks):
    if window:
        (sink_ref, q_ref, kp_ref, km_ref, kn_ref, kc_ref, vp_ref, vm_ref, vn_ref, vc_ref, o_ref) = refs
    else:
        sink_ref, q_ref, km_ref, vm_ref, o_ref = refs
    kvh = pl.program_id(2)
    if window:
        i = pl.program_id(1)
        k_all = jnp.concatenate([kp_ref[...], km_ref[...], kn_ref[...], kc_ref[0]], axis=0)
        v_all = jnp.concatenate([vp_ref[...], vm_ref[...], vn_ref[...], vc_ref[0]], axis=0)
        n_lat = tq + 2 * A_WINDOW
        qi = lax.broadcasted_iota(jnp.int32, (tq, k_all.shape[0]), 0)
        kj = lax.broadcasted_iota(jnp.int32, (tq, k_all.shape[0]), 1)
        in_seq = ((i > 0) | (kj >= A_WINDOW)) & ((i < n_blocks - 1) | (kj < tq + A_WINDOW))
        mask = (kj >= n_lat) | ((jnp.abs(kj - A_WINDOW - qi) <= A_WINDOW) & in_seq)
    else:
        k_all = km_ref[...].astype(BF16)
        v_all = vm_ref[...].astype(BF16)
    for g in range(groups):
        sl = slice(g * LANES, (g + 1) * LANES)
        q = q_ref[:, sl].astype(BF16)
        s = lax.dot_general(q, k_all, (((1,), (1,)), ((), ())), preferred_element_type=F32)
        if window:
            s = jnp.where(mask, s, NEG_INF)
        sink = sink_ref[0, kvh * groups + g] * LOG2E
        m = jnp.maximum(jnp.max(s, axis=-1, keepdims=True), sink)
        p = jnp.exp2(s - m)
        denom = jnp.sum(p, axis=-1, keepdims=True) + jnp.exp2(sink - m)
        o = jnp.dot(p.astype(BF16), v_all, preferred_element_type=F32)
        o_ref[:, sl] = (o / denom).astype(o_ref.dtype)


def _attn_a(qkv, sink, batch, seq_len, heads, kv_heads, ctx_k=None, ctx_v=None, tq=256):
    m = qkv.shape[0]
    groups = heads // kv_heads
    window = ctx_k is not None
    nb = seq_len // tq
    k_col, v_col = heads, heads + kv_heads
    r = tq // A_WINDOW
    rows128 = seq_len // A_WINDOW

    def main_rows(b, i):
        return b * nb + i

    def prev_rows(b, i):
        return b * rows128 + jnp.maximum(i * r - 1, 0)

    def next_rows(b, i):
        return b * rows128 + jnp.minimum((i + 1) * r, rows128 - 1)

    sink_spec = pl.BlockSpec(memory_space=pltpu.SMEM)
    q_spec = pl.BlockSpec((tq, groups * LANES), lambda b, i, k: (main_rows(b, i), k))

    def kv_specs(col):
        main = pl.BlockSpec((tq, LANES), lambda b, i, k: (main_rows(b, i), col + k))
        if not window:
            return [main]
        return [pl.BlockSpec((A_WINDOW, LANES), lambda b, i, k: (prev_rows(b, i), col + k)),
                main,
                pl.BlockSpec((A_WINDOW, LANES), lambda b, i, k: (next_rows(b, i), col + k)),
                pl.BlockSpec((1, ctx_k.shape[1], LANES), lambda b, i, k: (b, 0, k))]

    if window:
        args = [sink, qkv, qkv, qkv, qkv, ctx_k, qkv, qkv, qkv, ctx_v]
    else:
        args = [sink, qkv, qkv, qkv]
    return pl.pallas_call(
        functools.partial(_attn_a_kernel, groups=groups, window=window, tq=tq, n_blocks=nb),
        grid=(batch, nb, kv_heads),
        in_specs=[sink_spec, q_spec] + kv_specs(k_col) + kv_specs(v_col),
        out_specs=pl.BlockSpec((tq, groups * LANES), lambda b, i, k: (main_rows(b, i), k)),
        out_shape=jax.ShapeDtypeStruct((m, heads * LANES), BF16),
        compiler_params=_params("parallel", "arbitrary", "arbitrary"),
        name="attn_a_window" if window else "attn_a_ctx",
    )(*args)


def _lane_group_sum(p):
    return functools.reduce(jnp.add, [p[:, t * LANES:(t + 1) * LANES] for t in range(p.shape[1] // LANES)])


def _attn_b_kernel(*refs, has_ctx, out_scale, ck):
    if has_ctx:
        lam_ref, q_ref, k_ref, v_ref, kc_ref, vc_ref, sg_ref, o_ref = refs
    else:
        lam_ref, q_ref, k_ref, v_ref, sg_ref, o_ref = refs
    chunks = [(k_ref, v_ref, s0) for s0 in range(0, k_ref.shape[0], ck)]
    if has_ctx:
        chunks += [(kc_ref.at[0], vc_ref.at[0], s0) for s0 in range(0, kc_ref.shape[1], ck)]
    qs = [q_ref[:, c * LANES:(c + 1) * LANES].astype(BF16) for c in range(2)]
    state = [None, None]
    for kr, vr, s0 in chunks:
        v = vr[s0:s0 + ck, :].astype(BF16)
        for c in range(2):
            k = kr[s0:s0 + ck, c * LANES:(c + 1) * LANES].astype(BF16)
            s = lax.dot_general(qs[c], k, (((1,), (1,)), ((), ())), preferred_element_type=F32)
            row_max = jnp.max(s, axis=-1, keepdims=True)
            if state[c] is None:
                p = jnp.exp2(s - row_max)
                state[c] = (row_max, _lane_group_sum(p), jnp.dot(p.astype(BF16), v, preferred_element_type=F32))
            else:
                m_old, l_old, acc_old = state[c]
                m_new = jnp.maximum(m_old, row_max)
                alpha = jnp.exp2(m_old - m_new)
                p = jnp.exp2(s - m_new)
                state[c] = (m_new, alpha * l_old + _lane_group_sum(p),
                            alpha * acc_old + jnp.dot(p.astype(BF16), v, preferred_element_type=F32))
    outs = [acc / jnp.sum(l, axis=-1, keepdims=True) for _, l, acc in state]
    o = outs[0] - lam_ref[0, 0] * outs[1]
    o_ref[...] = (_rms(o) * sg_ref[...] * out_scale).astype(o_ref.dtype)


def _attn_b(qkv, lam, subln_g, out_scale, batch, seq_len, heads, ctx_k=None, ctx_v=None, tq=512, ck=512):
    m = qkv.shape[0]
    hd = 2 * LANES
    has_ctx = ctx_k is not None
    tq = _tile(seq_len, tq)
    ck = _tile(seq_len, ck, *([ctx_k.shape[1]] if has_ctx else []))
    nb = seq_len // tq
    in_specs = [pl.BlockSpec(memory_space=pltpu.SMEM),
                pl.BlockSpec((tq, hd), lambda b, h, i: (b * nb + i, h)),
                pl.BlockSpec((seq_len, hd), lambda b, h, i: (b, heads + h)),
                pl.BlockSpec((seq_len, hd), lambda b, h, i: (b, 2 * heads + h))]
    args = [lam, qkv, qkv, qkv]
    if has_ctx:
        in_specs += [pl.BlockSpec((1, ctx_k.shape[1], hd), lambda b, h, i: (b, 0, h))] * 2
        args += [ctx_k, ctx_v]
    in_specs.append(pl.BlockSpec((1, hd), lambda b, h, i: (0, 0)))
    args.append(subln_g)
    return pl.pallas_call(
        functools.partial(_attn_b_kernel, has_ctx=has_ctx, out_scale=out_scale, ck=ck),
        grid=(batch, heads, nb),
        in_specs=in_specs,
        out_specs=pl.BlockSpec((tq, hd), lambda b, h, i: (b * nb + i, h)),
        out_shape=jax.ShapeDtypeStruct((m, heads * hd), BF16),
        compiler_params=_params("parallel", "arbitrary", "arbitrary"),
        name="attn_b_latent" if has_ctx else "attn_b_ctx",
    )(*args)


def _proj_residual_kernel(a_ref, w_ref, x_ref, gate_ref, o_ref):
    y = jnp.dot(a_ref[...], w_ref[...], preferred_element_type=F32)
    o_ref[...] = x_ref[...] + gate_ref[0] * y


def _proj_residual(a, w, x, gate, seq_len, tm, tn):
    m, k = a.shape
    n = w.shape[1]
    per_batch = gate.shape[0] > 1
    tm = _tile(m, tm, seq_len) if per_batch else _tile(m, tm)
    tn = _tile(n, tn)
    mod_idx = _mod_index(per_batch, max(seq_len // tm, 1))
    gate_idx = lambda i, j: mod_idx(i, j)[:2] + (j,)
    return pl.pallas_call(
        _proj_residual_kernel,
        grid=(m // tm, n // tn),
        in_specs=[pl.BlockSpec((tm, k), lambda i, j: (i, 0)),
                  pl.BlockSpec((k, tn), lambda i, j: (0, j)),
                  pl.BlockSpec((tm, tn), lambda i, j: (i, j)),
                  pl.BlockSpec((1, 1, tn), gate_idx)],
        out_specs=pl.BlockSpec((tm, tn), lambda i, j: (i, j)),
        out_shape=jax.ShapeDtypeStruct((m, n), F32),
        compiler_params=_params("parallel", "arbitrary"),
        name="proj_residual",
    )(a, w, x, gate)


HALO = 16


def _ffn_kernel(xp_ref, x_ref, xn_ref, g_ref, sh_ref, sc_ref, gate_ref, wg_ref, wv_ref, cw_ref, cb_ref, wd_ref,
                o_ref, h_ref, acc_ref, *, seq_len, tm):
    i = pl.program_id(0)
    f = pl.program_id(1)

    @pl.when(f == 0)
    def _():
        g, sh, sc = g_ref[...], sh_ref[0], sc_ref[0]
        h_ref[0:HALO] = _adaln(xp_ref[...], g, sh, sc).astype(BF16)
        h_ref[HALO:HALO + tm] = _adaln(x_ref[...], g, sh, sc).astype(BF16)
        h_ref[HALO + tm:] = _adaln(xn_ref[...], g, sh, sc).astype(BF16)
        acc_ref[...] = jnp.zeros_like(acc_ref)

    gate_pre = jnp.dot(h_ref[...], wg_ref[...], preferred_element_type=F32)
    val = jnp.dot(h_ref[HALO:HALO + tm], wv_ref[...], preferred_element_type=F32)
    pos = (i * tm + lax.broadcasted_iota(jnp.int32, (tm, 1), 0)) & (seq_len - 1)
    up = jnp.where(pos != 0, gate_pre[HALO - 1:HALO - 1 + tm], 0.0)
    dn = jnp.where(pos != seq_len - 1, gate_pre[HALO + 1:HALO + 1 + tm], 0.0)
    gc = up * cw_ref[0:1] + gate_pre[HALO:HALO + tm] * cw_ref[1:2] + dn * cw_ref[2:3] + cb_ref[...]
    act = (gc / (1.0 + jnp.exp(-gc)) * val).astype(BF16)
    acc_ref[...] += jnp.dot(act, wd_ref[...], preferred_element_type=F32)

    @pl.when(f == pl.num_programs(1) - 1)
    def _():
        o_ref[...] = x_ref[...] + gate_ref[0] * acc_ref[...]


def _ffn(x, g, shift, scale, gate, w_up, conv_w, conv_b, w_down, seq_len, tm, tf):
    m, d = x.shape
    ff = w_down.shape[0]
    per_batch = shift.shape[0] > 1
    tm = _tile(m, tm, seq_len) if per_batch else _tile(m, tm)
    tf = _tile(ff, tf)
    assert (seq_len % tm == 0 or tm % seq_len == 0) and tm % HALO == 0
    assert seq_len & (seq_len - 1) == 0, "token position uses a power-of-two mask"
    mod_idx = _mod_index(per_batch, max(seq_len // tm, 1))
    hb = tm // HALO
    n_halo_blocks = m // HALO
    nf = ff // tf
    return pl.pallas_call(
        functools.partial(_ffn_kernel, seq_len=seq_len, tm=tm),
        grid=(m // tm, nf),
        in_specs=[pl.BlockSpec((HALO, d), lambda i, f: (jnp.maximum(i * hb - 1, 0), 0)),
                  pl.BlockSpec((tm, d), lambda i, f: (i, 0)),
                  pl.BlockSpec((HALO, d), lambda i, f: (jnp.minimum((i + 1) * hb, n_halo_blocks - 1), 0)),
                  pl.BlockSpec((1, d), lambda i, f: (0, 0)),
                  pl.BlockSpec((1, 1, d), mod_idx),
                  pl.BlockSpec((1, 1, d), mod_idx),
                  pl.BlockSpec((1, 1, d), mod_idx),
                  pl.BlockSpec((d, tf), lambda i, f: (0, f)),
                  pl.BlockSpec((d, tf), lambda i, f: (0, nf + f)),
                  pl.BlockSpec((3, tf), lambda i, f: (0, f)),
                  pl.BlockSpec((1, tf), lambda i, f: (0, f)),
                  pl.BlockSpec((tf, d), lambda i, f: (f, 0))],
        out_specs=pl.BlockSpec((tm, d), lambda i, f: (i, 0)),
        out_shape=jax.ShapeDtypeStruct((m, d), F32),
        scratch_shapes=[pltpu.VMEM((tm + 2 * HALO, d), BF16), pltpu.VMEM((tm, d), F32)],
        compiler_params=_params("parallel", "arbitrary"),
        name="conv_glu",
    )(x, x, x, g, shift, scale, gate, w_up, w_up, conv_w, conv_b, w_down)


def _rope_tables(n, dim):
    rows = n // GRID_W
    row = jnp.repeat(jnp.arange(rows, dtype=F32), GRID_W)
    col = jnp.tile(jnp.arange(GRID_W, dtype=F32), rows)
    half = dim // 2
    inv = ROPE_BASE ** (-jnp.arange(0, half, 2, dtype=F32) / half)
    ar, ac = row[:, None] * inv, col[:, None] * inv
    cos = jnp.concatenate([jnp.cos(ar), jnp.cos(ar), jnp.cos(ac), jnp.cos(ac)], axis=-1)
    sin = jnp.concatenate([-jnp.sin(ar), jnp.sin(ar), -jnp.sin(ac), jnp.sin(ac)], axis=-1)
    reps = MXU_COLS // dim
    return jnp.tile(cos, (1, reps)), jnp.tile(sin, (1, reps))


def kernel(x_prompt, x_sample, cache_a_k, cache_a_v, cache_b_k, cache_b_v, c, c_ctx, ada_w, ada_b, norm1_g, norm2_g, a_w_qkv, a_q_norm, a_k_norm, a_sink, a_w_o, b_w_qkv, b_q_norm, b_k_norm, b_lambda_q1, b_lambda_k1, b_lambda_q2, b_lambda_k2, b_subln, b_w_o, ffn_w_up, ffn_conv_w, ffn_conv_b, ffn_w_down):
    batch, seq, d = x_prompt.shape
    dec_batch, dec_seq, _ = x_sample.shape
    depth = ada_w.shape[0]
    a_heads = a_sink.shape[1]
    a_kvh, a_hd = cache_a_k.shape[3], cache_a_k.shape[4]
    b_heads, b_qk = cache_b_k.shape[3], cache_b_k.shape[5]
    b_vd = cache_b_v.shape[4]
    past = cache_a_k.shape[2]
    assert a_hd == LANES and b_qk == LANES and b_vd == 2 * LANES

    cond = jnp.concatenate([c, c_ctx[None, :], jnp.zeros((16 - dec_batch - 1, d), F32)], axis=0)
    mod = _modulation(cond, ada_w, ada_b).reshape(depth, 16, 6, d)
    cos, sin = _rope_tables(dec_seq, LANES)

    xp = x_prompt.reshape(batch * seq, d)
    xs = x_sample.reshape(dec_batch * dec_seq, d)
    a_k_list, a_v_list, b_k_list, b_v_list = [], [], [], []

    for l in range(depth):
        mod_s = [mod[l, :dec_batch, t][:, None, :] for t in range(6)]
        mod_p = [mod[l, dec_batch:dec_batch + 1, t][:, None, :] for t in range(6)]
        n1 = norm1_g[l][None, :]
        n2 = norm2_g[l][None, :]
        j = l // 2
        if l % 2 == 0:
            w_qkv = a_w_qkv[j].astype(BF16)
            w_o = a_w_o[j].astype(BF16)
            qn, kn = a_heads * a_hd, a_kvh * a_hd
            head_gain = jnp.concatenate([jnp.tile(a_q_norm[j] * (a_hd ** -0.5 * LOG2E), a_heads),
                                         jnp.tile(a_k_norm[j], a_kvh), jnp.ones((kn,), F32)])[None, :]
            sink = a_sink[j][None, :]
            qkv_p = _qkv(xp, n1, mod_p[0], mod_p[1], w_qkv, head_gain, qn + kn, seq, None, F32, tm=512, tn=512)
            op = _attn_a(qkv_p, sink, batch, seq, a_heads, a_kvh)
            qkv_s = _qkv(xs, n1, mod_s[0], mod_s[1], w_qkv, head_gain, qn + kn, dec_seq, (cos, sin), BF16,
                         tm=1024, tn=512)
            ctx_k = cache_a_k[:, j].reshape(dec_batch, past, kn).astype(BF16)
            ctx_v = cache_a_v[:, j].reshape(dec_batch, past, kn).astype(BF16)
            os_ = _attn_a(qkv_s, sink, dec_batch, dec_seq, a_heads, a_kvh, ctx_k, ctx_v)
            a_k_list.append(qkv_p[:, qn:qn + kn].reshape(batch, seq, a_kvh, a_hd))
            a_v_list.append(qkv_p[:, qn + kn:].reshape(batch, seq, a_kvh, a_hd))
        else:
            w_qkv = b_w_qkv[j].astype(BF16)
            w_o = b_w_o[j].astype(BF16)
            lambda_init = 0.8 - 0.6 * math.exp(-0.3 * l)
            lam = (jnp.exp(jnp.sum(b_lambda_q1[j] * b_lambda_k1[j])) - jnp.exp(jnp.sum(b_lambda_q2[j] * b_lambda_k2[j]))
                   + lambda_init).reshape(1, 1)
            qn = b_heads * 2 * b_qk
            head_gain = jnp.concatenate([jnp.tile(b_q_norm[j] * (b_qk ** -0.5 * LOG2E), 2 * b_heads),
                                         jnp.tile(b_k_norm[j], 2 * b_heads), jnp.ones((qn,), F32)])[None, :]
            subln = b_subln[j][None, :]
            qkv_p = _qkv(xp, n1, mod_p[0], mod_p[1], w_qkv, head_gain, 2 * qn, seq, None, F32, tm=512, tn=512)
            op = _attn_b(qkv_p, lam, subln, 1.0 - lambda_init, batch, seq, b_heads)
            qkv_s = _qkv(xs, n1, mod_s[0], mod_s[1], w_qkv, head_gain, 2 * qn, dec_seq, (cos, sin), BF16,
                         tm=1024, tn=512)
            ctx_k = cache_b_k[:, j].reshape(dec_batch, past, qn).astype(BF16)
            ctx_v = cache_b_v[:, j].reshape(dec_batch, past, qn).astype(BF16)
            os_ = _attn_b(qkv_s, lam, subln, 1.0 - lambda_init, dec_batch, dec_seq, b_heads, ctx_k, ctx_v)
            b_k_list.append(qkv_p[:, qn:2 * qn].reshape(batch, seq, b_heads, 2, b_qk))
            b_v_list.append(qkv_p[:, 2 * qn:].reshape(batch, seq, b_heads, b_vd))
        xp = _proj_residual(op, w_o, xp, mod_p[2], seq, tm=1024, tn=1024)
        xs = _proj_residual(os_, w_o, xs, mod_s[2], dec_seq, tm=1024, tn=1024)
        w_up = ffn_w_up[l].astype(BF16)
        w_down = ffn_w_down[l].astype(BF16)
        cb = ffn_conv_b[l][None, :]
        xp = _ffn(xp, n2, mod_p[3], mod_p[4], mod_p[5], w_up, ffn_conv_w[l], cb, w_down, seq, tm=512, tf=512)
        xs = _ffn(xs, n2, mod_s[3], mod_s[4], mod_s[5], w_up, ffn_conv_w[l], cb, w_down, dec_seq, tm=512, tf=512)

    return (xp.reshape(batch, seq, d), xs.reshape(dec_batch, dec_seq, d),
            jnp.stack(a_k_list, axis=1), jnp.stack(a_v_list, axis=1),
            jnp.stack(b_k_list, axis=1), jnp.stack(b_v_list, axis=1))
```

```python
import functools
import math

import jax
import jax.numpy as jnp
from jax import lax
from jax.experimental import pallas as pl
from jax.experimental.pallas import tpu as pltpu

GRID_W = 64
A_WINDOW = 128
ROPE_BASE = 10000.0
EPS = 1e-6
NEG_INF = -1e30
LANES = 128
MXU_COLS = 256
LOG2E = math.log2(math.e)
VMEM_LIMIT = 56 * 1024 * 1024

BF16 = jnp.bfloat16
F32 = jnp.float32


def _params(*sem):
    return pltpu.CompilerParams(dimension_semantics=sem, vmem_limit_bytes=VMEM_LIMIT)


def _tile(size, target, *also):
    t = min(target, size)
    while any(s % t for s in (size,) + also):
        t //= 2
    return t


def _rms(x):
    return x * lax.rsqrt(jnp.mean(x * x, axis=-1, keepdims=True) + EPS)


def _adaln(x, g, shift, scale):
    return _rms(x) * g * (1.0 + scale) + shift


def _mod_index(per_batch, tiles_per_seq):
    if per_batch:
        return lambda i, j: (i // tiles_per_seq, 0, 0)
    return lambda i, j: (0, 0, 0)


def _modulation_kernel(c_ref, w_ref, b_ref, o_ref):
    c = c_ref[...]
    s = (c / (1.0 + jnp.exp(-c))).astype(BF16)
    o_ref[0] = jnp.dot(s, w_ref[0].astype(BF16), preferred_element_type=F32) + b_ref[0]


def _modulation(cond, ada_w, ada_b, tn=1024):
    depth, d, n = ada_w.shape
    r = cond.shape[0]
    return pl.pallas_call(
        _modulation_kernel,
        grid=(depth, n // tn),
        in_specs=[pl.BlockSpec((r, d), lambda l, j: (0, 0)),
                  pl.BlockSpec((1, d, tn), lambda l, j: (l, 0, j)),
                  pl.BlockSpec((1, 1, tn), lambda l, j: (l, 0, j))],
        out_specs=pl.BlockSpec((1, r, tn), lambda l, j: (l, 0, j)),
        out_shape=jax.ShapeDtypeStruct((depth, r, n), F32),
        compiler_params=_params("arbitrary", "arbitrary"),
        name="modulation",
    )(cond, ada_w, ada_b.reshape(depth, 1, n))


def _head_matrices():
    i = jnp.arange(MXU_COLS)
    mean = jnp.where(i[:, None] // LANES == i[None, :] // LANES, 1.0 / LANES, 0.0)
    partner = jnp.where((i & 63) < 32, i + 32, i - 32)
    perm = (i[:, None] == partner[None, :])
    return mean.astype(BF16), perm.astype(BF16)


def _qkv_kernel(*refs, n_norm_tiles, rope):
    if rope:
        x_ref, g_ref, sh_ref, sc_ref, w_ref, hg_ref, mean_ref, perm_ref, cos_ref, sin_ref, o_ref, h_ref, inv_ref = refs
    else:
        x_ref, g_ref, sh_ref, sc_ref, w_ref, hg_ref, mean_ref, o_ref, h_ref, inv_ref = refs
    j = pl.program_id(1)

    @pl.when(j == 0)
    def _():
        inv_ref[...] = lax.rsqrt(jnp.mean(jnp.square(x_ref[...]), axis=-1, keepdims=True) + EPS)

    @pl.when(j == 0)
    def _():
        gain = g_ref[...] * (1.0 + sc_ref[0])
        h_ref[...] = (x_ref[...] * inv_ref[...] * gain + sh_ref[0]).astype(BF16)

    tn = o_ref.shape[1]
    y_all = jnp.dot(h_ref[...], w_ref[...], preferred_element_type=F32)

    @pl.when(j < n_norm_tiles)
    def _():
        for c in range(tn // MXU_COLS):
            sl = slice(c * MXU_COLS, (c + 1) * MXU_COLS)
            y = y_all[:, sl]
            ms = jnp.dot((y * y).astype(BF16), mean_ref[...], preferred_element_type=F32)
            z = y * hg_ref[:, sl]
            if rope:
                partner = jnp.dot(z.astype(BF16), perm_ref[...], preferred_element_type=F32)
                z = z * cos_ref[...] + partner * sin_ref[...]
            o_ref[:, sl] = (z * lax.rsqrt(ms + EPS)).astype(o_ref.dtype)

    @pl.when(j >= n_norm_tiles)
    def _():
        o_ref[...] = y_all.astype(o_ref.dtype)


def _qkv(x, g, shift, scale, w, head_gain, n_norm_cols, seq_len, rope_tabs, out_dtype, tm, tn):
    m, d = x.shape
    n = w.shape[1]
    per_batch = shift.shape[0] > 1
    tn = _tile(n, tn, n_norm_cols)
    assert tn % MXU_COLS == 0
    tm = _tile(m, tm, seq_len) if (per_batch or rope_tabs is not None) else _tile(m, tm)
    tiles_per_seq = max(seq_len // tm, 1)
    mod_idx = _mod_index(per_batch, tiles_per_seq)
    mean_mat, perm_mat = _head_matrices()
    const_spec = pl.BlockSpec((MXU_COLS, MXU_COLS), lambda i, j: (0, 0))
    in_specs = [pl.BlockSpec((tm, d), lambda i, j: (i, 0)),
                pl.BlockSpec((1, d), lambda i, j: (0, 0)),
                pl.BlockSpec((1, 1, d), mod_idx),
                pl.BlockSpec((1, 1, d), mod_idx),
                pl.BlockSpec((d, tn), lambda i, j: (0, j)),
                pl.BlockSpec((1, tn), lambda i, j: (0, j)),
                const_spec]
    args = [x, g, shift, scale, w, head_gain, mean_mat]
    if rope_tabs is not None:
        in_specs += [const_spec] + [pl.BlockSpec((tm, MXU_COLS), lambda i, j: (i % tiles_per_seq, 0))] * 2
        args += [perm_mat] + list(rope_tabs)
    return pl.pallas_call(
        functools.partial(_qkv_kernel, n_norm_tiles=n_norm_cols // tn, rope=rope_tabs is not None),
        grid=(m // tm, n // tn),
        in_specs=in_specs,
        out_specs=pl.BlockSpec((tm, tn), lambda i, j: (i, j)),
        out_shape=jax.ShapeDtypeStruct((m, n), out_dtype),
        scratch_shapes=[pltpu.VMEM((tm, d), BF16), pltpu.VMEM((tm, 1), F32)],
        compiler_params=_params("parallel", "arbitrary"),
        name="adaln_qkv",
    )(*args)


def _attn_a_kernel(*refs, groups, window, tq, n_blocks):
    if window:
        (sink_ref, q_ref, kp_ref, km_ref, kn_ref, kc_ref, vp_ref, vm_ref, vn_ref, vc_ref, o_ref) = refs
    else:
        sink_ref, q_ref, km_ref, vm_ref, o_ref = refs
    kvh = pl.program_id(2)
    if window:
        i = pl.program_id(1)
        k_all = jnp.concatenate([kp_ref[...], km_ref[...], kn_ref[...], kc_ref[0]], axis=0)
        v_all = jnp.concatenate([vp_ref[...], vm_ref[...], vn_ref[...], vc_ref[0]], axis=0)
        n_lat = tq + 2 * A_WINDOW
        qi = lax.broadcasted_iota(jnp.int32, (tq, k_all.shape[0]), 0)
        kj = lax.broadcasted_iota(jnp.int32, (tq, k_all.shape[0]), 1)
        in_seq = ((i > 0) | (kj >= A_WINDOW)) & ((i < n_blocks - 1) | (kj < tq + A_WINDOW))
        mask = (kj >= n_lat) | ((jnp.abs(kj - A_WINDOW - qi) <= A_WINDOW) & in_seq)
    else:
        k_all = km_ref[...].astype(BF16)
        v_all = vm_ref[...].astype(BF16)
    n_heads = sink_ref.shape[1] - 1

    def heads(bounded):
        for g in range(groups):
            sl = slice(g * LANES, (g + 1) * LANES)
            q = q_ref[:, sl].astype(BF16)
            s = lax.dot_general(q, k_all, (((1,), (1,)), ((), ())), preferred_element_type=F32)
            if window:
                s = jnp.where(mask, s, NEG_INF)
            sink = sink_ref[0, kvh * groups + g] * LOG2E
            if bounded:
                p = jnp.exp2(s)
                denom = jnp.sum(p, axis=-1, keepdims=True) + jnp.exp2(jnp.full((1, 1), sink, F32))
            else:
                m = jnp.maximum(jnp.max(s, axis=-1, keepdims=True), sink)
                p = jnp.exp2(s - m)
                denom = jnp.sum(p, axis=-1, keepdims=True) + jnp.exp2(sink - m)
            o = jnp.dot(p.astype(BF16), v_all, preferred_element_type=F32)
            o_ref[:, sl] = (o / denom).astype(o_ref.dtype)

    pl.when(sink_ref[0, n_heads] != 0.0)(functools.partial(heads, True))
    pl.when(sink_ref[0, n_heads] == 0.0)(functools.partial(heads, False))


def _attn_a(qkv, sink, batch, seq_len, heads, kv_heads, ctx_k=None, ctx_v=None, tq=256):
    m = qkv.shape[0]
    groups = heads // kv_heads
    window = ctx_k is not None
    nb = seq_len // tq
    k_col, v_col = heads, heads + kv_heads
    r = tq // A_WINDOW
    rows128 = seq_len // A_WINDOW

    def main_rows(b, i):
        return b * nb + i

    def prev_rows(b, i):
        return b * rows128 + jnp.maximum(i * r - 1, 0)

    def next_rows(b, i):
        return b * rows128 + jnp.minimum((i + 1) * r, rows128 - 1)

    sink_spec = pl.BlockSpec(memory_space=pltpu.SMEM)
    q_spec = pl.BlockSpec((tq, groups * LANES), lambda b, i, k: (main_rows(b, i), k))

    def kv_specs(col):
        main = pl.BlockSpec((tq, LANES), lambda b, i, k: (main_rows(b, i), col + k))
        if not window:
            return [main]
        return [pl.BlockSpec((A_WINDOW, LANES), lambda b, i, k: (prev_rows(b, i), col + k)),
                main,
                pl.BlockSpec((A_WINDOW, LANES), lambda b, i, k: (next_rows(b, i), col + k)),
                pl.BlockSpec((1, ctx_k.shape[1], LANES), lambda b, i, k: (b, 0, k))]

    if window:
        args = [sink, qkv, qkv, qkv, qkv, ctx_k, qkv, qkv, qkv, ctx_v]
    else:
        args = [sink, qkv, qkv, qkv]
    return pl.pallas_call(
        functools.partial(_attn_a_kernel, groups=groups, window=window, tq=tq, n_blocks=nb),
        grid=(batch, nb, kv_heads),
        in_specs=[sink_spec, q_spec] + kv_specs(k_col) + kv_specs(v_col),
        out_specs=pl.BlockSpec((tq, groups * LANES), lambda b, i, k: (main_rows(b, i), k)),
        out_shape=jax.ShapeDtypeStruct((m, heads * LANES), BF16),
        compiler_params=_params("parallel", "arbitrary", "arbitrary"),
        name="attn_a_window" if window else "attn_a_ctx",
    )(*args)


SCORE_BOUND = 60.0


def _scores_bounded(q_gain, k_gain, head_dim, ctx_k=None):
    q_norm = math.sqrt(head_dim) * jnp.max(jnp.abs(q_gain))
    k_norm = math.sqrt(head_dim) * jnp.max(jnp.abs(k_gain))
    if ctx_k is not None:
        k_norm = jnp.maximum(k_norm, jnp.sqrt(jnp.max(jnp.sum(jnp.square(ctx_k.astype(F32)), axis=-1))))
    return (q_norm * k_norm <= SCORE_BOUND).astype(F32)


def _lane_group_sum(p):
    return functools.reduce(jnp.add, [p[:, t * LANES:(t + 1) * LANES] for t in range(p.shape[1] // LANES)])


def _attn_b_kernel(*refs, has_ctx, out_scale, ck):
    if has_ctx:
        scal_ref, q_ref, k_ref, v_ref, kc_ref, vc_ref, sg_ref, o_ref = refs
    else:
        scal_ref, q_ref, k_ref, v_ref, sg_ref, o_ref = refs
    chunks = [(k_ref, v_ref, s0) for s0 in range(0, k_ref.shape[0], ck)]
    if has_ctx:
        chunks += [(kc_ref.at[0], vc_ref.at[0], s0) for s0 in range(0, kc_ref.shape[1], ck)]

    def scores(c, kr, s0):
        q = q_ref[:, c * LANES:(c + 1) * LANES].astype(BF16)
        k = kr[s0:s0 + ck, c * LANES:(c + 1) * LANES].astype(BF16)
        return lax.dot_general(q, k, (((1,), (1,)), ((), ())), preferred_element_type=F32)

    def finish(state):
        outs = [acc / jnp.sum(l, axis=-1, keepdims=True) for l, acc in state]
        o = outs[0] - scal_ref[0, 0] * outs[1]
        o_ref[...] = (_rms(o) * sg_ref[...] * out_scale).astype(o_ref.dtype)

    @pl.when(scal_ref[0, 1] != 0.0)
    def _():
        state = [None, None]
        for kr, vr, s0 in chunks:
            v = vr[s0:s0 + ck, :].astype(BF16)
            for c in range(2):
                p = jnp.exp2(scores(c, kr, s0))
                l, acc = _lane_group_sum(p), jnp.dot(p.astype(BF16), v, preferred_element_type=F32)
                state[c] = (l, acc) if state[c] is None else (state[c][0] + l, state[c][1] + acc)
        finish(state)

    @pl.when(scal_ref[0, 1] == 0.0)
    def _():
        state = [None, None]
        for kr, vr, s0 in chunks:
            v = vr[s0:s0 + ck, :].astype(BF16)
            for c in range(2):
                s = scores(c, kr, s0)
                row_max = jnp.max(s, axis=-1, keepdims=True)
                if state[c] is None:
                    p = jnp.exp2(s - row_max)
                    state[c] = (row_max, _lane_group_sum(p), jnp.dot(p.astype(BF16), v, preferred_element_type=F32))
                else:
                    m_old, l_old, acc_old = state[c]
                    m_new = jnp.maximum(m_old, row_max)
                    alpha = jnp.exp2(m_old - m_new)
                    p = jnp.exp2(s - m_new)
                    state[c] = (m_new, alpha * l_old + _lane_group_sum(p),
                                alpha * acc_old + jnp.dot(p.astype(BF16), v, preferred_element_type=F32))
        finish([(l, acc) for _, l, acc in state])


def _attn_b(qkv, scal, subln_g, out_scale, batch, seq_len, heads, ctx_k=None, ctx_v=None, tq=512, ck=512):
    m = qkv.shape[0]
    hd = 2 * LANES
    has_ctx = ctx_k is not None
    tq = _tile(seq_len, tq)
    ck = _tile(seq_len, ck, *([ctx_k.shape[1]] if has_ctx else []))
    nb = seq_len // tq
    in_specs = [pl.BlockSpec(memory_space=pltpu.SMEM),
                pl.BlockSpec((tq, hd), lambda b, h, i: (b * nb + i, h)),
                pl.BlockSpec((seq_len, hd), lambda b, h, i: (b, heads + h)),
                pl.BlockSpec((seq_len, hd), lambda b, h, i: (b, 2 * heads + h))]
    args = [scal, qkv, qkv, qkv]
    if has_ctx:
        in_specs += [pl.BlockSpec((1, ctx_k.shape[1], hd), lambda b, h, i: (b, 0, h))] * 2
        args += [ctx_k, ctx_v]
    in_specs.append(pl.BlockSpec((1, hd), lambda b, h, i: (0, 0)))
    args.append(subln_g)
    return pl.pallas_call(
        functools.partial(_attn_b_kernel, has_ctx=has_ctx, out_scale=out_scale, ck=ck),
        grid=(batch, heads, nb),
        in_specs=in_specs,
        out_specs=pl.BlockSpec((tq, hd), lambda b, h, i: (b * nb + i, h)),
        out_shape=jax.ShapeDtypeStruct((m, heads * hd), BF16),
        compiler_params=_params("parallel", "arbitrary", "arbitrary"),
        name="attn_b_latent" if has_ctx else "attn_b_ctx",
    )(*args)


def _proj_residual_kernel(a_ref, w_ref, x_ref, gate_ref, o_ref):
    y = jnp.dot(a_ref[...], w_ref[...], preferred_element_type=F32)
    o_ref[...] = x_ref[...] + gate_ref[0] * y


def _proj_residual(a, w, x, gate, seq_len, tm, tn):
    m, k = a.shape
    n = w.shape[1]
    per_batch = gate.shape[0] > 1
    tm = _tile(m, tm, seq_len) if per_batch else _tile(m, tm)
    tn = _tile(n, tn)
    mod_idx = _mod_index(per_batch, max(seq_len // tm, 1))
    gate_idx = lambda i, j: mod_idx(i, j)[:2] + (j,)
    return pl.pallas_call(
        _proj_residual_kernel,
        grid=(m // tm, n // tn),
        in_specs=[pl.BlockSpec((tm, k), lambda i, j: (i, 0)),
                  pl.BlockSpec((k, tn), lambda i, j: (0, j)),
                  pl.BlockSpec((tm, tn), lambda i, j: (i, j)),
                  pl.BlockSpec((1, 1, tn), gate_idx)],
        out_specs=pl.BlockSpec((tm, tn), lambda i, j: (i, j)),
        out_shape=jax.ShapeDtypeStruct((m, n), F32),
        compiler_params=_params("parallel", "arbitrary"),
        name="proj_residual",
    )(a, w, x, gate)


HALO = 16


def _ffn_kernel(xp_ref, x_ref, xn_ref, g_ref, sh_ref, sc_ref, gate_ref, wg_ref, wv_ref, cw_ref, cb_ref, wd_ref,
                o_ref, h_ref, acc_ref, inv_ref, *, seq_len, tm):
    i = pl.program_id(0)
    f = pl.program_id(1)

    @pl.when(f == 0)
    def _():
        inv_ref[...] = lax.rsqrt(jnp.mean(jnp.square(x_ref[...]), axis=-1, keepdims=True) + EPS)

    @pl.when(f == 0)
    def _():
        g, sh, sc = g_ref[...], sh_ref[0], sc_ref[0]
        h_ref[0:HALO] = _adaln(xp_ref[...], g, sh, sc).astype(BF16)
        h_ref[HALO:HALO + tm] = (x_ref[...] * inv_ref[...] * (g * (1.0 + sc)) + sh).astype(BF16)
        h_ref[HALO + tm:] = _adaln(xn_ref[...], g, sh, sc).astype(BF16)
        acc_ref[...] = jnp.zeros_like(acc_ref)

    gate_pre = jnp.dot(h_ref[...], wg_ref[...], preferred_element_type=F32)
    val = jnp.dot(h_ref[HALO:HALO + tm], wv_ref[...], preferred_element_type=F32)
    pos = (i * tm + lax.broadcasted_iota(jnp.int32, (tm, 1), 0)) & (seq_len - 1)
    up = jnp.where(pos != 0, gate_pre[HALO - 1:HALO - 1 + tm], 0.0)
    dn = jnp.where(pos != seq_len - 1, gate_pre[HALO + 1:HALO + 1 + tm], 0.0)
    gc = up * cw_ref[0:1] + gate_pre[HALO:HALO + tm] * cw_ref[1:2] + dn * cw_ref[2:3] + cb_ref[...]
    act = (gc / (1.0 + jnp.exp(-gc)) * val).astype(BF16)
    acc_ref[...] += jnp.dot(act, wd_ref[...], preferred_element_type=F32)

    @pl.when(f == pl.num_programs(1) - 1)
    def _():
        o_ref[...] = x_ref[...] + gate_ref[0] * acc_ref[...]


def _ffn(x, g, shift, scale, gate, w_up, conv_w, conv_b, w_down, seq_len, tm, tf):
    m, d = x.shape
    ff = w_down.shape[0]
    per_batch = shift.shape[0] > 1
    tm = _tile(m, tm, seq_len) if per_batch else _tile(m, tm)
    tf = _tile(ff, tf)
    assert (seq_len % tm == 0 or tm % seq_len == 0) and tm % HALO == 0
    assert seq_len & (seq_len - 1) == 0, "token position uses a power-of-two mask"
    mod_idx = _mod_index(per_batch, max(seq_len // tm, 1))
    hb = tm // HALO
    n_halo_blocks = m // HALO
    nf = ff // tf
    return pl.pallas_call(
        functools.partial(_ffn_kernel, seq_len=seq_len, tm=tm),
        grid=(m // tm, nf),
        in_specs=[pl.BlockSpec((HALO, d), lambda i, f: (jnp.maximum(i * hb - 1, 0), 0)),
                  pl.BlockSpec((tm, d), lambda i, f: (i, 0)),
                  pl.BlockSpec((HALO, d), lambda i, f: (jnp.minimum((i + 1) * hb, n_halo_blocks - 1), 0)),
                  pl.BlockSpec((1, d), lambda i, f: (0, 0)),
                  pl.BlockSpec((1, 1, d), mod_idx),
                  pl.BlockSpec((1, 1, d), mod_idx),
                  pl.BlockSpec((1, 1, d), mod_idx),
                  pl.BlockSpec((d, tf), lambda i, f: (0, f)),
                  pl.BlockSpec((d, tf), lambda i, f: (0, nf + f)),
                  pl.BlockSpec((3, tf), lambda i, f: (0, f)),
                  pl.BlockSpec((1, tf), lambda i, f: (0, f)),
                  pl.BlockSpec((tf, d), lambda i, f: (f, 0))],
        out_specs=pl.BlockSpec((tm, d), lambda i, f: (i, 0)),
        out_shape=jax.ShapeDtypeStruct((m, d), F32),
        scratch_shapes=[pltpu.VMEM((tm + 2 * HALO, d), BF16), pltpu.VMEM((tm, d), F32), pltpu.VMEM((tm, 1), F32)],
        compiler_params=_params("parallel", "arbitrary"),
        name="conv_glu",
    )(x, x, x, g, shift, scale, gate, w_up, w_up, conv_w, conv_b, w_down)


def _rope_tables(n, dim):
    rows = n // GRID_W
    row = jnp.repeat(jnp.arange(rows, dtype=F32), GRID_W)
    col = jnp.tile(jnp.arange(GRID_W, dtype=F32), rows)
    half = dim // 2
    inv = ROPE_BASE ** (-jnp.arange(0, half, 2, dtype=F32) / half)
    ar, ac = row[:, None] * inv, col[:, None] * inv
    cos = jnp.concatenate([jnp.cos(ar), jnp.cos(ar), jnp.cos(ac), jnp.cos(ac)], axis=-1)
    sin = jnp.concatenate([-jnp.sin(ar), jnp.sin(ar), -jnp.sin(ac), jnp.sin(ac)], axis=-1)
    reps = MXU_COLS // dim
    return jnp.tile(cos, (1, reps)), jnp.tile(sin, (1, reps))


def kernel(x_prompt, x_sample, cache_a_k, cache_a_v, cache_b_k, cache_b_v, c, c_ctx, ada_w, ada_b, norm1_g, norm2_g, a_w_qkv, a_q_norm, a_k_norm, a_sink, a_w_o, b_w_qkv, b_q_norm, b_k_norm, b_lambda_q1, b_lambda_k1, b_lambda_q2, b_lambda_k2, b_subln, b_w_o, ffn_w_up, ffn_conv_w, ffn_conv_b, ffn_w_down):
    batch, seq, d = x_prompt.shape
    dec_batch, dec_seq, _ = x_sample.shape
    depth = ada_w.shape[0]
    a_heads = a_sink.shape[1]
    a_kvh, a_hd = cache_a_k.shape[3], cache_a_k.shape[4]
    b_heads, b_qk = cache_b_k.shape[3], cache_b_k.shape[5]
    b_vd = cache_b_v.shape[4]
    past = cache_a_k.shape[2]
    assert a_hd == LANES and b_qk == LANES and b_vd == 2 * LANES

    cond = jnp.concatenate([c, c_ctx[None, :], jnp.zeros((16 - dec_batch - 1, d), F32)], axis=0)
    mod = _modulation(cond, ada_w, ada_b).reshape(depth, 16, 6, d)
    cos, sin = _rope_tables(dec_seq, LANES)

    xp = x_prompt.reshape(batch * seq, d)
    xs = x_sample.reshape(dec_batch * dec_seq, d)
    a_k_list, a_v_list, b_k_list, b_v_list = [], [], [], []

    for l in range(depth):
        mod_s = [mod[l, :dec_batch, t][:, None, :] for t in range(6)]
        mod_p = [mod[l, dec_batch:dec_batch + 1, t][:, None, :] for t in range(6)]
        n1 = norm1_g[l][None, :]
        n2 = norm2_g[l][None, :]
        j = l // 2
        if l % 2 == 0:
            w_qkv = a_w_qkv[j].astype(BF16)
            w_o = a_w_o[j].astype(BF16)
            qn, kn = a_heads * a_hd, a_kvh * a_hd
            q_gain = a_q_norm[j] * (a_hd ** -0.5 * LOG2E)
            head_gain = jnp.concatenate([jnp.tile(q_gain, a_heads),
                                         jnp.tile(a_k_norm[j], a_kvh), jnp.ones((kn,), F32)])[None, :]
            sink_ok = (jnp.max(jnp.abs(a_sink[j])) * LOG2E <= SCORE_BOUND).astype(F32)
            sink_p = jnp.concatenate([a_sink[j], sink_ok * _scores_bounded(q_gain, a_k_norm[j], a_hd)[None]])[None, :]
            sink_s = jnp.concatenate([a_sink[j], sink_ok * _scores_bounded(q_gain, a_k_norm[j], a_hd,
                                                                          cache_a_k[:, j])[None]])[None, :]
            qkv_p = _qkv(xp, n1, mod_p[0], mod_p[1], w_qkv, head_gain, qn + kn, seq, None, F32, tm=512, tn=512)
            op = _attn_a(qkv_p, sink_p, batch, seq, a_heads, a_kvh)
            qkv_s = _qkv(xs, n1, mod_s[0], mod_s[1], w_qkv, head_gain, qn + kn, dec_seq, (cos, sin), BF16,
                         tm=1024, tn=512)
            ctx_k = cache_a_k[:, j].reshape(dec_batch, past, kn).astype(BF16)
            ctx_v = cache_a_v[:, j].reshape(dec_batch, past, kn).astype(BF16)
            os_ = _attn_a(qkv_s, sink_s, dec_batch, dec_seq, a_heads, a_kvh, ctx_k, ctx_v)
            a_k_list.append(qkv_p[:, qn:qn + kn].reshape(batch, seq, a_kvh, a_hd))
            a_v_list.append(qkv_p[:, qn + kn:].reshape(batch, seq, a_kvh, a_hd))
        else:
            w_qkv = b_w_qkv[j].astype(BF16)
            w_o = b_w_o[j].astype(BF16)
            lambda_init = 0.8 - 0.6 * math.exp(-0.3 * l)
            lam = (jnp.exp(jnp.sum(b_lambda_q1[j] * b_lambda_k1[j])) - jnp.exp(jnp.sum(b_lambda_q2[j] * b_lambda_k2[j]))
                   + lambda_init)
            qn = b_heads * 2 * b_qk
            q_gain = b_q_norm[j] * (b_qk ** -0.5 * LOG2E)
            head_gain = jnp.concatenate([jnp.tile(q_gain, 2 * b_heads),
                                         jnp.tile(b_k_norm[j], 2 * b_heads), jnp.ones((qn,), F32)])[None, :]
            subln = b_subln[j][None, :]
            scal_p = jnp.stack([lam, _scores_bounded(q_gain, b_k_norm[j], b_qk)])[None, :]
            scal_s = jnp.stack([lam, _scores_bounded(q_gain, b_k_norm[j], b_qk, cache_b_k[:, j])])[None, :]
            qkv_p = _qkv(xp, n1, mod_p[0], mod_p[1], w_qkv, head_gain, 2 * qn, seq, None, F32, tm=512, tn=512)
            op = _attn_b(qkv_p, scal_p, subln, 1.0 - lambda_init, batch, seq, b_heads)
            qkv_s = _qkv(xs, n1, mod_s[0], mod_s[1], w_qkv, head_gain, 2 * qn, dec_seq, (cos, sin), BF16,
                         tm=1024, tn=512)
            ctx_k = cache_b_k[:, j].reshape(dec_batch, past, qn).astype(BF16)
            ctx_v = cache_b_v[:, j].reshape(dec_batch, past, qn).astype(BF16)
            os_ = _attn_b(qkv_s, scal_s, subln, 1.0 - lambda_init, dec_batch, dec_seq, b_heads, ctx_k, ctx_v)
            b_k_list.append(qkv_p[:, qn:2 * qn].reshape(batch, seq, b_heads, 2, b_qk))
            b_v_list.append(qkv_p[:, 2 * qn:].reshape(batch, seq, b_heads, b_vd))
        xp = _proj_residual(op, w_o, xp, mod_p[2], seq, tm=1024, tn=1024)
        xs = _proj_residual(os_, w_o, xs, mod_s[2], dec_seq, tm=1024, tn=1024)
        w_up = ffn_w_up[l].astype(BF16)
        w_down = ffn_w_down[l].astype(BF16)
        cb = ffn_conv_b[l][None, :]
        xp = _ffn(xp, n2, mod_p[3], mod_p[4], mod_p[5], w_up, ffn_conv_w[l], cb, w_down, seq, tm=512, tf=512)
        xs = _ffn(xs, n2, mod_s[3], mod_s[4], mod_s[5], w_up, ffn_conv_w[l], cb, w_down, dec_seq, tm=512, tf=512)

    return (xp.reshape(batch, seq, d), xs.reshape(dec_batch, dec_seq, d),
            jnp.stack(a_k_list, axis=1), jnp.stack(a_v_list, axis=1),
            jnp.stack(b_k_list, axis=1), jnp.stack(b_v_list, axis=1))
```

```python
import functools
import math

import jax
import jax.numpy as jnp
from jax import lax
from jax.experimental import pallas as pl
from jax.experimental.pallas import tpu as pltpu

GRID_W = 64
A_WINDOW = 128
ROPE_BASE = 10000.0
EPS = 1e-6
NEG_INF = -1e30
LANES = 128
MXU_COLS = 256
LOG2E = math.log2(math.e)
VMEM_LIMIT = 56 * 1024 * 1024

BF16 = jnp.bfloat16
F32 = jnp.float32


def _params(*sem):
    return pltpu.CompilerParams(dimension_semantics=sem, vmem_limit_bytes=VMEM_LIMIT)


def _tile(size, target, *also):
    t = min(target, size)
    while any(s % t for s in (size,) + also):
        t //= 2
    return t


def _rms(x):
    return x * lax.rsqrt(jnp.mean(x * x, axis=-1, keepdims=True) + EPS)


def _adaln(x, g, shift, scale):
    return _rms(x) * g * (1.0 + scale) + shift


def _mod_index(per_batch, tiles_per_seq):
    if per_batch:
        return lambda i, j: (i // tiles_per_seq, 0, 0)
    return lambda i, j: (0, 0, 0)


def _modulation_kernel(c_ref, w_ref, b_ref, o_ref):
    c = c_ref[...]
    s = (c / (1.0 + jnp.exp(-c))).astype(BF16)
    o_ref[0] = jnp.dot(s, w_ref[0].astype(BF16), preferred_element_type=F32) + b_ref[0]


def _modulation(cond, ada_w, ada_b, tn=1024):
    depth, d, n = ada_w.shape
    r = cond.shape[0]
    return pl.pallas_call(
        _modulation_kernel,
        grid=(depth, n // tn),
        in_specs=[pl.BlockSpec((r, d), lambda l, j: (0, 0)),
                  pl.BlockSpec((1, d, tn), lambda l, j: (l, 0, j)),
                  pl.BlockSpec((1, 1, tn), lambda l, j: (l, 0, j))],
        out_specs=pl.BlockSpec((1, r, tn), lambda l, j: (l, 0, j)),
        out_shape=jax.ShapeDtypeStruct((depth, r, n), F32),
        compiler_params=_params("arbitrary", "arbitrary"),
        name="modulation",
    )(cond, ada_w, ada_b.reshape(depth, 1, n))


def _head_matrices():
    i = jnp.arange(MXU_COLS)
    mean = jnp.where(i[:, None] // LANES == i[None, :] // LANES, 1.0 / LANES, 0.0)
    partner = jnp.where((i & 63) < 32, i + 32, i - 32)
    perm = (i[:, None] == partner[None, :])
    return mean.astype(BF16), perm.astype(BF16)


def _qkv_kernel(*refs, n_norm_cols, rope):
    if rope:
        x_ref, g_ref, sh_ref, sc_ref, w_ref, hg_ref, mean_ref, perm_ref, cos_ref, sin_ref, o_ref, h_ref, inv_ref = refs
    else:
        x_ref, g_ref, sh_ref, sc_ref, w_ref, hg_ref, mean_ref, o_ref, h_ref, inv_ref = refs
    j = pl.program_id(1)

    @pl.when(j == 0)
    def _():
        inv_ref[...] = lax.rsqrt(jnp.mean(jnp.square(x_ref[...]), axis=-1, keepdims=True) + EPS)

    @pl.when(j == 0)
    def _():
        gain = g_ref[...] * (1.0 + sc_ref[0])
        h_ref[...] = (x_ref[...] * inv_ref[...] * gain + sh_ref[0]).astype(BF16)

    tn = o_ref.shape[1]
    subs = tn // MXU_COLS
    y_all = jnp.dot(h_ref[...], w_ref[...], preferred_element_type=F32)

    def epilogue(n_normed):
        for c in range(subs):
            sl = slice(c * MXU_COLS, (c + 1) * MXU_COLS)
            y = y_all[:, sl]
            if c < n_normed:
                ms = jnp.dot((y * y).astype(BF16), mean_ref[...], preferred_element_type=F32)
                z = y * hg_ref[:, sl]
                if rope:
                    partner = jnp.dot(z.astype(BF16), perm_ref[...], preferred_element_type=F32)
                    z = z * cos_ref[...] + partner * sin_ref[...]
                y = z * lax.rsqrt(ms + EPS)
            o_ref[:, sl] = y.astype(o_ref.dtype)

    full_tiles, part_subs = n_norm_cols // tn, (n_norm_cols % tn) // MXU_COLS
    pl.when(j < full_tiles)(functools.partial(epilogue, subs))
    if part_subs:
        pl.when(j == full_tiles)(functools.partial(epilogue, part_subs))
    pl.when(j >= full_tiles + (1 if part_subs else 0))(functools.partial(epilogue, 0))


def _qkv(x, g, shift, scale, w, layer, head_gain, n_norm_cols, seq_len, rope_tabs, out_dtype, tm, tn):
    m, d = x.shape
    n = w.shape[2]
    per_batch = shift.shape[0] > 1
    tn = _tile(n, tn)
    assert tn % MXU_COLS == 0 and n_norm_cols % MXU_COLS == 0
    tm = _tile(m, tm, seq_len) if (per_batch or rope_tabs is not None) else _tile(m, tm)
    tiles_per_seq = max(seq_len // tm, 1)
    mod_idx = _mod_index(per_batch, tiles_per_seq)
    mean_mat, perm_mat = _head_matrices()
    const_spec = pl.BlockSpec((MXU_COLS, MXU_COLS), lambda i, j: (0, 0))
    in_specs = [pl.BlockSpec((tm, d), lambda i, j: (i, 0)),
                pl.BlockSpec((1, d), lambda i, j: (0, 0)),
                pl.BlockSpec((1, 1, d), mod_idx),
                pl.BlockSpec((1, 1, d), mod_idx),
                pl.BlockSpec((None, d, tn), lambda i, j: (layer, 0, j)),
                pl.BlockSpec((1, tn), lambda i, j: (0, j)),
                const_spec]
    args = [x, g, shift, scale, w, head_gain, mean_mat]
    if rope_tabs is not None:
        in_specs += [const_spec] + [pl.BlockSpec((tm, MXU_COLS), lambda i, j: (i % tiles_per_seq, 0))] * 2
        args += [perm_mat] + list(rope_tabs)
    return pl.pallas_call(
        functools.partial(_qkv_kernel, n_norm_cols=n_norm_cols, rope=rope_tabs is not None),
        grid=(m // tm, n // tn),
        in_specs=in_specs,
        out_specs=pl.BlockSpec((tm, tn), lambda i, j: (i, j)),
        out_shape=jax.ShapeDtypeStruct((m, n), out_dtype),
        scratch_shapes=[pltpu.VMEM((tm, d), BF16), pltpu.VMEM((tm, 1), F32)],
        compiler_params=_params("parallel", "arbitrary"),
        name="adaln_qkv",
    )(*args)


def _attn_a_kernel(*refs, groups, window, tq, n_blocks):
    if window:
        (sink_ref, q_ref, kp_ref, km_ref, kn_ref, kc_ref, vp_ref, vm_ref, vn_ref, vc_ref, o_ref) = refs
    else:
        sink_ref, q_ref, km_ref, vm_ref, o_ref = refs
    kvh = pl.program_id(2)
    if window:
        i = pl.program_id(1)
        k_all = jnp.concatenate([kp_ref[...], km_ref[...], kn_ref[...], kc_ref[0]], axis=0)
        v_all = jnp.concatenate([vp_ref[...], vm_ref[...], vn_ref[...], vc_ref[0]], axis=0)
        n_lat = tq + 2 * A_WINDOW
        qi = lax.broadcasted_iota(jnp.int32, (tq, k_all.shape[0]), 0)
        kj = lax.broadcasted_iota(jnp.int32, (tq, k_all.shape[0]), 1)
        in_seq = ((i > 0) | (kj >= A_WINDOW)) & ((i < n_blocks - 1) | (kj < tq + A_WINDOW))
        mask = (kj >= n_lat) | ((jnp.abs(kj - A_WINDOW - qi) <= A_WINDOW) & in_seq)
    else:
        k_all = km_ref[...].astype(BF16)
        v_all = vm_ref[...].astype(BF16)
    n_heads = sink_ref.shape[1] - 1

    def heads(bounded):
        for g in range(groups):
            sl = slice(g * LANES, (g + 1) * LANES)
            q = q_ref[:, sl].astype(BF16)
            s = lax.dot_general(q, k_all, (((1,), (1,)), ((), ())), preferred_element_type=F32)
            if window:
                s = jnp.where(mask, s, NEG_INF)
            sink = sink_ref[0, kvh * groups + g] * LOG2E
            if bounded:
                p = jnp.exp2(s)
                denom = jnp.sum(p, axis=-1, keepdims=True) + jnp.exp2(jnp.full((1, 1), sink, F32))
            else:
                m = jnp.maximum(jnp.max(s, axis=-1, keepdims=True), sink)
                p = jnp.exp2(s - m)
                denom = jnp.sum(p, axis=-1, keepdims=True) + jnp.exp2(sink - m)
            o = jnp.dot(p.astype(BF16), v_all, preferred_element_type=F32)
            o_ref[:, sl] = (o / denom).astype(o_ref.dtype)

    pl.when(sink_ref[0, n_heads] != 0.0)(functools.partial(heads, True))
    pl.when(sink_ref[0, n_heads] == 0.0)(functools.partial(heads, False))


def _attn_a(qkv, sink, batch, seq_len, heads, kv_heads, ctx_k=None, ctx_v=None, tq=256):
    m = qkv.shape[0]
    groups = heads // kv_heads
    window = ctx_k is not None
    tq = _tile(seq_len, tq)
    nb = seq_len // tq
    k_col, v_col = heads, heads + kv_heads
    r = tq // A_WINDOW
    rows128 = seq_len // A_WINDOW

    def main_rows(b, i):
        return b * nb + i

    def prev_rows(b, i):
        return b * rows128 + jnp.maximum(i * r - 1, 0)

    def next_rows(b, i):
        return b * rows128 + jnp.minimum((i + 1) * r, rows128 - 1)

    sink_spec = pl.BlockSpec(memory_space=pltpu.SMEM)
    q_spec = pl.BlockSpec((tq, groups * LANES), lambda b, i, k: (main_rows(b, i), k))

    def kv_specs(col):
        main = pl.BlockSpec((tq, LANES), lambda b, i, k: (main_rows(b, i), col + k))
        if not window:
            return [main]
        return [pl.BlockSpec((A_WINDOW, LANES), lambda b, i, k: (prev_rows(b, i), col + k)),
                main,
                pl.BlockSpec((A_WINDOW, LANES), lambda b, i, k: (next_rows(b, i), col + k)),
                pl.BlockSpec((1, ctx_k.shape[1], LANES), lambda b, i, k: (b, 0, k))]

    if window:
        args = [sink, qkv, qkv, qkv, qkv, ctx_k, qkv, qkv, qkv, ctx_v]
    else:
        args = [sink, qkv, qkv, qkv]
    return pl.pallas_call(
        functools.partial(_attn_a_kernel, groups=groups, window=window, tq=tq, n_blocks=nb),
        grid=(batch, nb, kv_heads),
        in_specs=[sink_spec, q_spec] + kv_specs(k_col) + kv_specs(v_col),
        out_specs=pl.BlockSpec((tq, groups * LANES), lambda b, i, k: (main_rows(b, i), k)),
        out_shape=jax.ShapeDtypeStruct((m, heads * LANES), BF16),
        compiler_params=_params("parallel", "arbitrary", "arbitrary"),
        name="attn_a_window" if window else "attn_a_ctx",
    )(*args)


SCORE_BOUND = 60.0


def _scores_bounded(q_gain, k_gain, head_dim, ctx_k=None):
    q_norm = math.sqrt(head_dim) * jnp.max(jnp.abs(q_gain))
    k_norm = math.sqrt(head_dim) * jnp.max(jnp.abs(k_gain))
    if ctx_k is not None:
        k_norm = jnp.maximum(k_norm, jnp.sqrt(jnp.max(jnp.sum(jnp.square(ctx_k.astype(F32)), axis=-1))))
    return (q_norm * k_norm <= SCORE_BOUND).astype(F32)


def _lane_group_sum(p):
    return functools.reduce(jnp.add, [p[:, t * LANES:(t + 1) * LANES] for t in range(p.shape[1] // LANES)])


def _attn_b_kernel(*refs, has_ctx, out_scale, ck):
    if has_ctx:
        scal_ref, q_ref, k_ref, v_ref, kc_ref, vc_ref, sg_ref, o_ref = refs
    else:
        scal_ref, q_ref, k_ref, v_ref, sg_ref, o_ref = refs
    chunks =[(k_ref, v_ref, s0) for s0 in range(0, k_ref.shape[0], ck)]
    if has_ctx:
        chunks += [(kc_ref.at[0], vc_ref.at[0], s0) for s0 in range(0, kc_ref.shape[1], ck)]

    def scores(c, kr, s0):
        q = q_ref[:, c * LANES:(c + 1) * LANES].astype(BF16)
        k = kr[s0:s0 + ck, c * LANES:(c + 1) * LANES].astype(BF16)
        return lax.dot_general(q, k, (((1,), (1,)), ((), ())), preferred_element_type=F32)

    def finish(state):
        outs = [acc / jnp.sum(l, axis=-1, keepdims=True) for l, acc in state]
        o = outs[0] - scal_ref[0, 0] * outs[1]
        o_ref[...] = (_rms(o) * sg_ref[...] * out_scale).astype(o_ref.dtype)

    @pl.when(scal_ref[0, 1] != 0.0)
    def _():
        state = [None, None]
        for kr, vr, s0 in chunks:
            v = vr[s0:s0 + ck, :].astype(BF16)
            for c in range(2):
                p = jnp.exp2(scores(c, kr, s0))
                l, acc = _lane_group_sum(p), jnp.dot(p.astype(BF16), v, preferred_element_type=F32)
                state[c] = (l, acc) if state[c] is None else (state[c][0] + l, state[c][1] + acc)
        finish(state)

    @pl.when(scal_ref[0, 1] == 0.0)
    def _():
        state = [None, None]
        for kr, vr, s0 in chunks:
            v = vr[s0:s0 + ck, :].astype(BF16)
            for c in range(2):
                s = scores(c, kr, s0)
                row_max = jnp.max(s, axis=-1, keepdims=True)
                if state[c] is None:
                    p = jnp.exp2(s - row_max)
                    state[c] = (row_max, _lane_group_sum(p), jnp.dot(p.astype(BF16), v, preferred_element_type=F32))
                else:
                    m_old, l_old, acc_old = state[c]
                    m_new = jnp.maximum(m_old, row_max)
                    alpha = jnp.exp2(m_old - m_new)
                    p = jnp.exp2(s - m_new)
                    state[c] = (m_new, alpha * l_old + _lane_group_sum(p),
                                alpha * acc_old + jnp.dot(p.astype(BF16), v, preferred_element_type=F32))
        finish([(l, acc) for _, l, acc in state])


def _attn_b(qkv, scal, subln_g, out_scale, batch, seq_len, heads, ctx_k=None, ctx_v=None, tq=1024, ck=512):
    m = qkv.shape[0]
    hd = 2 * LANES
    has_ctx = ctx_k is not None
    tq = _tile(seq_len, tq)
    ck = _tile(seq_len, ck, *([ctx_k.shape[1]] if has_ctx else []))
    nb = seq_len // tq
    in_specs = [pl.BlockSpec(memory_space=pltpu.SMEM),
                pl.BlockSpec((tq, hd), lambda b, h, i: (b * nb + i, h)),
                pl.BlockSpec((seq_len, hd), lambda b, h, i: (b, heads + h)),
                pl.BlockSpec((seq_len, hd), lambda b, h, i: (b, 2 * heads + h))]
    args = [scal, qkv, qkv, qkv]
    if has_ctx:
        in_specs += [pl.BlockSpec((1, ctx_k.shape[1], hd), lambda b, h, i: (b, 0, h))] * 2
        args += [ctx_k, ctx_v]
    in_specs.append(pl.BlockSpec((1, hd), lambda b, h, i: (0, 0)))
    args.append(subln_g)
    return pl.pallas_call(
        functools.partial(_attn_b_kernel, has_ctx=has_ctx, out_scale=out_scale, ck=ck),
        grid=(batch, heads, nb),
        in_specs=in_specs,
        out_specs=pl.BlockSpec((tq, hd), lambda b, h, i: (b * nb + i, h)),
        out_shape=jax.ShapeDtypeStruct((m, heads * hd), BF16),
        compiler_params=_params("parallel", "arbitrary", "arbitrary"),
        name="attn_b_latent" if has_ctx else "attn_b_ctx",
    )(*args)


def _proj_residual_kernel(a_ref, w_ref, x_ref, gate_ref, o_ref):
    y = jnp.dot(a_ref[...], w_ref[...], preferred_element_type=F32)
    o_ref[...] = x_ref[...] + gate_ref[0] * y


def _proj_residual(a, w, layer, x, gate, seq_len, tm, tn):
    m, k = a.shape
    n = w.shape[2]
    per_batch = gate.shape[0] > 1
    tm = _tile(m, tm, seq_len) if per_batch else _tile(m, tm)
    tn = _tile(n, tn)
    mod_idx = _mod_index(per_batch, max(seq_len // tm, 1))
    gate_idx = lambda i, j: mod_idx(i, j)[:2] + (j,)
    return pl.pallas_call(
        _proj_residual_kernel,
        grid=(m // tm, n // tn),
        in_specs=[pl.BlockSpec((tm, k), lambda i, j: (i, 0)),
                  pl.BlockSpec((None, k, tn), lambda i, j: (layer, 0, j)),
                  pl.BlockSpec((tm, tn), lambda i, j: (i, j)),
                  pl.BlockSpec((1, 1, tn), gate_idx)],
        out_specs=pl.BlockSpec((tm, tn), lambda i, j: (i, j)),
        out_shape=jax.ShapeDtypeStruct((m, n), F32),
        compiler_params=_params("parallel", "arbitrary"),
        name="proj_residual",
    )(a, w, x, gate)


HALO = 16


def _ffn_kernel(xp_ref, x_ref, xn_ref, g_ref, sh_ref, sc_ref, gate_ref, wg_ref, wv_ref, cw_ref, cb_ref, wd_ref,
                o_ref, h_ref, acc_ref, inv_ref, *, seq_len, tm):
    i = pl.program_id(0)
    f = pl.program_id(1)

    @pl.when(f == 0)
    def _():
        inv_ref[...] = lax.rsqrt(jnp.mean(jnp.square(x_ref[...]), axis=-1, keepdims=True) + EPS)

    @pl.when(f == 0)
    def _():
        g, sh, sc = g_ref[...], sh_ref[0], sc_ref[0]
        h_ref[0:HALO] = _adaln(xp_ref[...], g, sh, sc).astype(BF16)
        h_ref[HALO:HALO + tm] = (x_ref[...] * inv_ref[...] * (g * (1.0 + sc)) + sh).astype(BF16)
        h_ref[HALO + tm:] = _adaln(xn_ref[...], g, sh, sc).astype(BF16)
        acc_ref[...] = jnp.zeros_like(acc_ref)

    gate_pre = jnp.dot(h_ref[...], wg_ref[...], preferred_element_type=F32)
    val = jnp.dot(h_ref[HALO:HALO + tm], wv_ref[...], preferred_element_type=F32)
    pos = (i * tm + lax.broadcasted_iota(jnp.int32, (tm, 1), 0)) & (seq_len - 1)
    up = jnp.where(pos != 0, gate_pre[HALO - 1:HALO - 1 + tm], 0.0)
    dn = jnp.where(pos != seq_len - 1, gate_pre[HALO + 1:HALO + 1 + tm], 0.0)
    gc = up * cw_ref[0:1] + gate_pre[HALO:HALO + tm] * cw_ref[1:2] + dn * cw_ref[2:3] + cb_ref[...]
    act = (gc / (1.0 + jnp.exp(-gc)) * val).astype(BF16)
    acc_ref[...] += jnp.dot(act, wd_ref[...], preferred_element_type=F32)

    @pl.when(f == pl.num_programs(1) - 1)
    def _():
        o_ref[...] = x_ref[...] + gate_ref[0] * acc_ref[...]


def _ffn(x, g, shift, scale, gate, w_up, conv_w, conv_b, w_down, layer, seq_len, tm, tf):
    m, d = x.shape
    ff = w_down.shape[1]
    per_batch = shift.shape[0] > 1
    tm = _tile(m, tm, seq_len) if per_batch else _tile(m, tm)
    tf = _tile(ff, tf)
    assert (seq_len % tm == 0 or tm % seq_len == 0) and tm % HALO == 0
    assert seq_len & (seq_len - 1) == 0, "token position uses a power-of-two mask"
    mod_idx = _mod_index(per_batch, max(seq_len // tm, 1))
    hb = tm // HALO
    n_halo_blocks = m // HALO
    nf = ff // tf
    return pl.pallas_call(
        functools.partial(_ffn_kernel, seq_len=seq_len, tm=tm),
        grid=(m // tm, nf),
        in_specs=[pl.BlockSpec((HALO, d), lambda i, f: (jnp.maximum(i * hb - 1, 0), 0)),
                  pl.BlockSpec((tm, d), lambda i, f: (i, 0)),
                  pl.BlockSpec((HALO, d), lambda i, f: (jnp.minimum((i + 1) * hb, n_halo_blocks - 1), 0)),
                  pl.BlockSpec((1, d), lambda i, f: (0, 0)),
                  pl.BlockSpec((1, 1, d), mod_idx),
                  pl.BlockSpec((1, 1, d), mod_idx),
                  pl.BlockSpec((1, 1, d), mod_idx),
                  pl.BlockSpec((None, d, tf), lambda i, f: (layer, 0, f)),
                  pl.BlockSpec((None, d, tf), lambda i, f: (layer, 0, nf + f)),
                  pl.BlockSpec((3, tf), lambda i, f: (0, f)),
                  pl.BlockSpec((1, tf), lambda i, f: (0, f)),
                  pl.BlockSpec((None, tf, d), lambda i, f: (layer, f, 0))],
        out_specs=pl.BlockSpec((tm, d), lambda i, f: (i, 0)),
        out_shape=jax.ShapeDtypeStruct((m, d), F32),
        scratch_shapes=[pltpu.VMEM((tm + 2 * HALO, d), BF16), pltpu.VMEM((tm, d), F32), pltpu.VMEM((tm, 1), F32)],
        compiler_params=_params("parallel", "arbitrary"),
        name="conv_glu",
    )(x, x, x, g, shift, scale, gate, w_up, w_up, conv_w, conv_b, w_down)


def _rope_tables(n, dim):
    rows = n // GRID_W
    row = jnp.repeat(jnp.arange(rows, dtype=F32), GRID_W)
    col = jnp.tile(jnp.arange(GRID_W, dtype=F32), rows)
    half = dim // 2
    inv = ROPE_BASE ** (-jnp.arange(0, half, 2, dtype=F32) / half)
    ar, ac = row[:, None] * inv, col[:, None] * inv
    cos = jnp.concatenate([jnp.cos(ar), jnp.cos(ar), jnp.cos(ac), jnp.cos(ac)], axis=-1)
    sin = jnp.concatenate([-jnp.sin(ar), jnp.sin(ar), -jnp.sin(ac), jnp.sin(ac)], axis=-1)
    reps = MXU_COLS // dim
    return jnp.tile(cos, (1, reps)), jnp.tile(sin, (1, reps))


def kernel(x_prompt, x_sample, cache_a_k, cache_a_v, cache_b_k, cache_b_v, c, c_ctx, ada_w, ada_b, norm1_g, norm2_g, a_w_qkv, a_q_norm, a_k_norm, a_sink, a_w_o, b_w_qkv, b_q_norm, b_k_norm, b_lambda_q1, b_lambda_k1, b_lambda_q2, b_lambda_k2, b_subln, b_w_o, ffn_w_up, ffn_conv_w, ffn_conv_b, ffn_w_down):
    batch, seq, d = x_prompt.shape
    dec_batch, dec_seq, _ = x_sample.shape
    depth = ada_w.shape[0]
    a_heads = a_sink.shape[1]
    a_kvh, a_hd = cache_a_k.shape[3], cache_a_k.shape[4]
    b_heads, b_qk = cache_b_k.shape[3], cache_b_k.shape[5]
    b_vd = cache_b_v.shape[4]
    past = cache_a_k.shape[2]
    assert a_hd == LANES and b_qk == LANES and b_vd == 2 * LANES

    cond = jnp.concatenate([c, c_ctx[None, :], jnp.zeros((16 - dec_batch - 1, d), F32)], axis=0)
    mod = _modulation(cond, ada_w, ada_b).reshape(depth, 16, 6, d)
    cos, sin = _rope_tables(dec_seq, LANES)

    xp = x_prompt.reshape(batch * seq, d)
    xs = x_sample.reshape(dec_batch * dec_seq, d)
    a_k_list, a_v_list, b_k_list, b_v_list = [], [], [], []
    a_wqkv, a_wo, b_wqkv, b_wo, w_up, w_down = [w.astype(BF16) for w in
                                               (a_w_qkv, a_w_o, b_w_qkv, b_w_o, ffn_w_up, ffn_w_down)]

    for l in range(depth):
        mod_s = [mod[l, :dec_batch, t][:, None, :] for t in range(6)]
        mod_p = [mod[l, dec_batch:dec_batch + 1, t][:, None, :] for t in range(6)]
        n1 = norm1_g[l][None, :]
        n2 = norm2_g[l][None, :]
        j = l // 2
        if l % 2 == 0:
            w_qkv, w_o = a_wqkv, a_wo
            qn, kn = a_heads * a_hd, a_kvh * a_hd
            q_gain = a_q_norm[j] * (a_hd ** -0.5 * LOG2E)
            head_gain = jnp.concatenate([jnp.tile(q_gain, a_heads),
                                         jnp.tile(a_k_norm[j], a_kvh), jnp.ones((kn,), F32)])[None, :]
            sink_ok = (jnp.max(jnp.abs(a_sink[j])) * LOG2E <= SCORE_BOUND).astype(F32)
            sink_p = jnp.concatenate([a_sink[j], sink_ok * _scores_bounded(q_gain, a_k_norm[j], a_hd)[None]])[None, :]
            sink_s = jnp.concatenate([a_sink[j], sink_ok * _scores_bounded(q_gain, a_k_norm[j], a_hd,
                                                                          cache_a_k[:, j])[None]])[None, :]
            qkv_p = _qkv(xp, n1, mod_p[0], mod_p[1], w_qkv, j, head_gain, qn + kn, seq, None, F32, tm=512, tn=1024)
            op = _attn_a(qkv_p, sink_p, batch, seq, a_heads, a_kvh)
            qkv_s = _qkv(xs, n1, mod_s[0], mod_s[1], w_qkv, j, head_gain, qn + kn, dec_seq, (cos, sin), BF16,
                         tm=1024, tn=1024)
            ctx_k = cache_a_k[:, j].reshape(dec_batch, past, kn).astype(BF16)
            ctx_v = cache_a_v[:, j].reshape(dec_batch, past, kn).astype(BF16)
            os_ = _attn_a(qkv_s, sink_s, dec_batch, dec_seq, a_heads, a_kvh, ctx_k, ctx_v)
            a_k_list.append(qkv_p[:, qn:qn + kn].reshape(batch, seq, a_kvh, a_hd))
            a_v_list.append(qkv_p[:, qn + kn:].reshape(batch, seq, a_kvh, a_hd))
        else:
            w_qkv, w_o = b_wqkv, b_wo
            lambda_init = 0.8 - 0.6 * math.exp(-0.3 * l)
            lam = (jnp.exp(jnp.sum(b_lambda_q1[j] * b_lambda_k1[j])) - jnp.exp(jnp.sum(b_lambda_q2[j] * b_lambda_k2[j]))
                   + lambda_init)
            qn = b_heads * 2 * b_qk
            q_gain = b_q_norm[j] * (b_qk ** -0.5 * LOG2E)
            head_gain = jnp.concatenate([jnp.tile(q_gain, 2 * b_heads),
                                         jnp.tile(b_k_norm[j], 2 * b_heads), jnp.ones((qn,), F32)])[None, :]
            subln = b_subln[j][None, :]
            scal_p = jnp.stack([lam, _scores_bounded(q_gain, b_k_norm[j], b_qk)])[None, :]
            scal_s = jnp.stack([lam, _scores_bounded(q_gain, b_k_norm[j], b_qk, cache_b_k[:, j])])[None, :]
            qkv_p = _qkv(xp, n1, mod_p[0], mod_p[1], w_qkv, j, head_gain, 2 * qn, seq, None, F32, tm=512, tn=1024)
            op = _attn_b(qkv_p, scal_p, subln, 1.0 - lambda_init, batch, seq, b_heads)
            qkv_s = _qkv(xs, n1, mod_s[0], mod_s[1], w_qkv, j, head_gain, 2 * qn, dec_seq, (cos, sin), BF16,
                         tm=1024, tn=1024)
            ctx_k = cache_b_k[:, j].reshape(dec_batch, past, qn).astype(BF16)
            ctx_v = cache_b_v[:, j].reshape(dec_batch, past, qn).astype(BF16)
            os_ = _attn_b(qkv_s, scal_s, subln, 1.0 - lambda_init, dec_batch, dec_seq, b_heads, ctx_k, ctx_v)
            b_k_list.append(qkv_p[:, qn:2 * qn].reshape(batch, seq, b_heads, 2, b_qk))
            b_v_list.append(qkv_p[:, 2 * qn:].reshape(batch, seq, b_heads, b_vd))
        xp = _proj_residual(op, w_o, j, xp, mod_p[2], seq, tm=512, tn=2048)
        xs = _proj_residual(os_, w_o, j, xs, mod_s[2], dec_seq, tm=512, tn=2048)
        cb = ffn_conv_b[l][None, :]
        xp = _ffn(xp, n2, mod_p[3], mod_p[4], mod_p[5], w_up, ffn_conv_w[l], cb, w_down, l, seq, tm=512, tf=512)
        xs = _ffn(xs, n2, mod_s[3], mod_s[4], mod_s[5], w_up, ffn_conv_w[l], cb, w_down, l, dec_seq, tm=512, tf=512)

    return (xp.reshape(batch, seq, d), xs.reshape(dec_batch, dec_seq, d),
            jnp.stack(a_k_list, axis=1), jnp.stack(a_v_list, axis=1),
            jnp.stack(b_k_list, axis=1), jnp.stack(b_v_list, axis=1))
```

```python
import functools
import math

import jax
import jax.numpy as jnp
from jax import lax
from jax.experimental import pallas as pl
from jax.experimental.pallas import tpu as pltpu

GRID_W = 64
A_WINDOW = 128
ROPE_BASE = 10000.0
EPS = 1e-6
NEG_INF = -1e30
LANES = 128
MXU_COLS = 256
LOG2E = math.log2(math.e)
VMEM_LIMIT = 56 * 1024 * 1024

BF16 = jnp.bfloat16
F32 = jnp.float32


def _params(*sem):
    return pltpu.CompilerParams(dimension_semantics=sem, vmem_limit_bytes=VMEM_LIMIT)


def _tile(size, target, *also):
    t = min(target, size)
    while any(s % t for s in (size,) + also):
        t //= 2
    return t


def _rms(x):
    return x * lax.rsqrt(jnp.mean(x * x, axis=-1, keepdims=True) + EPS)


def _adaln(x, g, shift, scale):
    return _rms(x) * g * (1.0 + scale) + shift


def _mod_index(per_batch, tiles_per_seq):
    if per_batch:
        return lambda i, j: (i // tiles_per_seq, 0, 0)
    return lambda i, j: (0, 0, 0)


def _modulation_kernel(c_ref, w_ref, b_ref, o_ref):
    c = c_ref[...]
    s = (c / (1.0 + jnp.exp(-c))).astype(BF16)
    o_ref[0] = jnp.dot(s, w_ref[0].astype(BF16), preferred_element_type=F32) + b_ref[0]


def _modulation(cond, ada_w, ada_b, tn=1024):
    depth, d, n = ada_w.shape
    r = cond.shape[0]
    return pl.pallas_call(
        _modulation_kernel,
        grid=(depth, n // tn),
        in_specs=[pl.BlockSpec((r, d), lambda l, j: (0, 0)),
                  pl.BlockSpec((1, d, tn), lambda l, j: (l, 0, j)),
                  pl.BlockSpec((1, 1, tn), lambda l, j: (l, 0, j))],
        out_specs=pl.BlockSpec((1, r, tn), lambda l, j: (l, 0, j)),
        out_shape=jax.ShapeDtypeStruct((depth, r, n), F32),
        compiler_params=_params("arbitrary", "arbitrary"),
        name="modulation",
    )(cond, ada_w, ada_b.reshape(depth, 1, n))


def _head_matrices():
    i = jnp.arange(MXU_COLS)
    mean = jnp.where(i[:, None] // LANES == i[None, :] // LANES, 1.0 / LANES, 0.0)
    partner = jnp.where((i & 63) < 32, i + 32, i - 32)
    perm = (i[:, None] == partner[None, :])
    return mean.astype(BF16), perm.astype(BF16)


def _qkv_kernel(*refs, n_norm_cols, rope):
    if rope:
        x_ref, g_ref, sh_ref, sc_ref, w_ref, hg_ref, mean_ref, perm_ref, cos_ref, sin_ref, o_ref, h_ref, inv_ref = refs
    else:
        x_ref, g_ref, sh_ref, sc_ref, w_ref, hg_ref, mean_ref, o_ref, h_ref, inv_ref = refs
    j = pl.program_id(1)

    @pl.when(j == 0)
    def _():
        inv_ref[...] = lax.rsqrt(jnp.mean(jnp.square(x_ref[...]), axis=-1, keepdims=True) + EPS)

    @pl.when(j == 0)
    def _():
        gain = g_ref[...] * (1.0 + sc_ref[0])
        h_ref[...] = (x_ref[...] * inv_ref[...] * gain + sh_ref[0]).astype(BF16)

    tn = o_ref.shape[1]
    subs = tn // MXU_COLS
    y_all = jnp.dot(h_ref[...], w_ref[...], preferred_element_type=F32)

    def epilogue(n_normed):
        for c in range(subs):
            sl = slice(c * MXU_COLS, (c + 1) * MXU_COLS)
            y = y_all[:, sl]
            if c < n_normed:
                ms = jnp.dot((y * y).astype(BF16), mean_ref[...], preferred_element_type=F32)
                z = y * hg_ref[:, sl]
                if rope:
                    partner = jnp.dot(z.astype(BF16), perm_ref[...], preferred_element_type=F32)
                    z = z * cos_ref[...] + partner * sin_ref[...]
                y = z * lax.rsqrt(ms + EPS)
            o_ref[:, sl] = y.astype(o_ref.dtype)

    full_tiles, part_subs = n_norm_cols // tn, (n_norm_cols % tn) // MXU_COLS
    pl.when(j < full_tiles)(functools.partial(epilogue, subs))
    if part_subs:
        pl.when(j == full_tiles)(functools.partial(epilogue, part_subs))
    pl.when(j >= full_tiles + (1 if part_subs else 0))(functools.partial(epilogue, 0))


def _chunk_columns(w, t):
    layers, k, n = w.shape
    return w.astype(BF16).reshape(layers, k, n // t, t).transpose(0, 2, 1, 3)


def _qkv(x, g, shift, scale, w, layer, head_gain, n_norm_cols, seq_len, rope_tabs, out_dtype, tm):
    m, d = x.shape
    tn = w.shape[3]
    n = w.shape[1] * tn
    per_batch = shift.shape[0] > 1
    assert tn % MXU_COLS == 0 and n_norm_cols % MXU_COLS == 0
    tm = _tile(m, tm, seq_len) if (per_batch or rope_tabs is not None) else _tile(m, tm)
    tiles_per_seq = max(seq_len // tm, 1)
    mod_idx = _mod_index(per_batch, tiles_per_seq)
    mean_mat, perm_mat = _head_matrices()
    const_spec = pl.BlockSpec((MXU_COLS, MXU_COLS), lambda i, j: (0, 0))
    in_specs = [pl.BlockSpec((tm, d), lambda i, j: (i, 0)),
                pl.BlockSpec((1, d), lambda i, j: (0, 0)),
                pl.BlockSpec((1, 1, d), mod_idx),
                pl.BlockSpec((1, 1, d), mod_idx),
                pl.BlockSpec((None, None, d, tn), lambda i, j: (layer, j, 0, 0)),
                pl.BlockSpec((1, tn), lambda i, j: (0, j)),
                const_spec]
    args = [x, g, shift, scale, w, head_gain, mean_mat]
    if rope_tabs is not None:
        in_specs += [const_spec] + [pl.BlockSpec((tm, MXU_COLS), lambda i, j: (i % tiles_per_seq, 0))] * 2
        args += [perm_mat] + list(rope_tabs)
    return pl.pallas_call(
        functools.partial(_qkv_kernel, n_norm_cols=n_norm_cols, rope=rope_tabs is not None),
        grid=(m // tm, n // tn),
        in_specs=in_specs,
        out_specs=pl.BlockSpec((tm, tn), lambda i, j: (i, j)),
        out_shape=jax.ShapeDtypeStruct((m, n), out_dtype),
        scratch_shapes=[pltpu.VMEM((tm, d), BF16), pltpu.VMEM((tm, 1), F32)],
        compiler_params=_params("parallel", "arbitrary"),
        name="adaln_qkv",
    )(*args)


def _attn_a_kernel(*refs, groups, window, tq, n_blocks):
    if window:
        (sink_ref, q_ref, kp_ref, km_ref, kn_ref, kc_ref, vp_ref, vm_ref, vn_ref, vc_ref, o_ref) = refs
    else:
        sink_ref, q_ref, km_ref, vm_ref, o_ref = refs
    kvh = pl.program_id(2)
    if window:
        i = pl.program_id(1)
        k_all = jnp.concatenate([kp_ref[...], km_ref[...], kn_ref[...], kc_ref[0]], axis=0)
        v_all = jnp.concatenate([vp_ref[...], vm_ref[...], vn_ref[...], vc_ref[0]], axis=0)
        n_lat = tq + 2 * A_WINDOW
        qi = lax.broadcasted_iota(jnp.int32, (tq, k_all.shape[0]), 0)
        kj = lax.broadcasted_iota(jnp.int32, (tq, k_all.shape[0]), 1)
        in_seq = ((i > 0) | (kj >= A_WINDOW)) & ((i < n_blocks - 1) | (kj < tq + A_WINDOW))
        mask = (kj >= n_lat) | ((jnp.abs(kj - A_WINDOW - qi) <= A_WINDOW) & in_seq)
    else:
        k_all = km_ref[...].astype(BF16)
        v_all = vm_ref[...].astype(BF16)
    n_heads = sink_ref.shape[1] - 1

    def heads(bounded):
        for g in range(groups):
            sl = slice(g * LANES, (g + 1) * LANES)
            q = q_ref[:, sl].astype(BF16)
            s = lax.dot_general(q, k_all, (((1,), (1,)), ((), ())), preferred_element_type=F32)
            if window:
                s = jnp.where(mask, s, NEG_INF)
            sink = sink_ref[0, kvh * groups + g] * LOG2E
            if bounded:
                p = jnp.exp2(s)
                denom = jnp.sum(p, axis=-1, keepdims=True) + jnp.exp2(jnp.full((1, 1), sink, F32))
            else:
                m = jnp.maximum(jnp.max(s, axis=-1, keepdims=True), sink)
                p = jnp.exp2(s - m)
                denom = jnp.sum(p, axis=-1, keepdims=True) + jnp.exp2(sink - m)
            o = jnp.dot(p.astype(BF16), v_all, preferred_element_type=F32)
            o_ref[:, sl] = (o / denom).astype(o_ref.dtype)

    pl.when(sink_ref[0, n_heads] != 0.0)(functools.partial(heads, True))
    pl.when(sink_ref[0, n_heads] == 0.0)(functools.partial(heads, False))


def _attn_a(qkv, sink, batch, seq_len, heads, kv_heads, ctx_k=None, ctx_v=None, tq=256):
    m = qkv.shape[0]
    groups = heads // kv_heads
    window = ctx_k is not None
    tq = _tile(seq_len, tq)
    nb = seq_len // tq
    k_col, v_col = heads, heads + kv_heads
    r = tq // A_WINDOW
    rows128 = seq_len // A_WINDOW

    def main_rows(b, i):
        return b * nb + i

    def prev_rows(b, i):
        return b * rows128 + jnp.maximum(i * r - 1, 0)

    def next_rows(b, i):
        return b * rows128 + jnp.minimum((i + 1) * r, rows128 - 1)

    sink_spec = pl.BlockSpec(memory_space=pltpu.SMEM)
    q_spec = pl.BlockSpec((tq, groups * LANES), lambda b, i, k: (main_rows(b, i), k))

    def kv_specs(col):
        main = pl.BlockSpec((tq, LANES), lambda b, i, k: (main_rows(b, i), col + k))
        if not window:
            return [main]
        return [pl.BlockSpec((A_WINDOW, LANES), lambda b, i, k: (prev_rows(b, i), col + k)),
                main,
                pl.BlockSpec((A_WINDOW, LANES), lambda b, i, k: (next_rows(b, i), col + k)),
                pl.BlockSpec((1, ctx_k.shape[1], LANES), lambda b, i, k: (b, 0, k))]

    if window:
        args = [sink, qkv, qkv, qkv, qkv, ctx_k, qkv, qkv, qkv, ctx_v]
    else:
        args = [sink, qkv, qkv, qkv]
    return pl.pallas_call(
        functools.partial(_attn_a_kernel, groups=groups, window=window, tq=tq, n_blocks=nb),
        grid=(batch, nb, kv_heads),
        in_specs=[sink_spec, q_spec] + kv_specs(k_col) + kv_specs(v_col),
        out_specs=pl.BlockSpec((tq, groups * LANES), lambda b, i, k: (main_rows(b, i), k)),
        out_shape=jax.ShapeDtypeStruct((m, heads * LANES), BF16),
        compiler_params=_params("parallel", "arbitrary", "arbitrary"),
        name="attn_a_window" if window else "attn_a_ctx",
    )(*args)


SCORE_BOUND = 60.0


def _scores_bounded(q_gain, k_gain, head_dim, ctx_k=None):
    q_norm = math.sqrt(head_dim) * jnp.max(jnp.abs(q_gain))
    k_norm = math.sqrt(head_dim) * jnp.max(jnp.abs(k_gain))
    if ctx_k is not None:
        k_norm = jnp.maximum(k_norm, jnp.sqrt(jnp.max(jnp.sum(jnp.square(ctx_k.astype(F32)), axis=-1))))
    return (q_norm * k_norm <= SCORE_BOUND).astype(F32)


def _lane_group_sum(p):
    return functools.reduce(jnp.add, [p[:, t * LANES:(t + 1) * LANES] for t in range(p.shape[1] // LANES)])


def _attn_b_kernel(*refs, has_ctx, out_scale, ck):
    if has_ctx:
        scal_ref, q_ref, k_ref, v_ref, kc_ref, vc_ref, sg_ref, o_ref = refs
    else:
        scal_ref, q_ref, k_ref, v_ref, sg_ref, o_ref = refs
    chunks =[(k_ref, v_ref, s0) for s0 in range(0, k_ref.shape[0], ck)]
    if has_ctx:
        chunks += [(kc_ref.at[0], vc_ref.at[0], s0) for s0 in range(0, kc_ref.shape[1], ck)]

    def scores(c, kr, s0):
        q = q_ref[:, c * LANES:(c + 1) * LANES].astype(BF16)
        k = kr[s0:s0 + ck, c * LANES:(c + 1) * LANES].astype(BF16)
        return lax.dot_general(q, k, (((1,), (1,)), ((), ())), preferred_element_type=F32)

    def finish(state):
        outs = [acc / jnp.sum(l, axis=-1, keepdims=True) for l, acc in state]
        o = outs[0] - scal_ref[0, 0] * outs[1]
        o_ref[...] = (_rms(o) * sg_ref[...] * out_scale).astype(o_ref.dtype)

    @pl.when(scal_ref[0, 1] != 0.0)
    def _():
        state = [None, None]
        for kr, vr, s0 in chunks:
            v = vr[s0:s0 + ck, :].astype(BF16)
            for c in range(2):
                p = jnp.exp2(scores(c, kr, s0))
                l, acc = _lane_group_sum(p), jnp.dot(p.astype(BF16), v, preferred_element_type=F32)
                state[c] = (l, acc) if state[c] is None else (state[c][0] + l, state[c][1] + acc)
        finish(state)

    @pl.when(scal_ref[0, 1] == 0.0)
    def _():
        state = [None, None]
        for kr, vr, s0 in chunks:
            v = vr[s0:s0 + ck, :].astype(BF16)
            for c in range(2):
                s = scores(c, kr, s0)
                row_max = jnp.max(s, axis=-1, keepdims=True)
                if state[c] is None:
                    p = jnp.exp2(s - row_max)
                    state[c] = (row_max, _lane_group_sum(p), jnp.dot(p.astype(BF16), v, preferred_element_type=F32))
                else:
                    m_old, l_old, acc_old = state[c]
                    m_new = jnp.maximum(m_old, row_max)
                    alpha = jnp.exp2(m_old - m_new)
                    p = jnp.exp2(s - m_new)
                    state[c] = (m_new, alpha * l_old + _lane_group_sum(p),
                                alpha * acc_old + jnp.dot(p.astype(BF16), v, preferred_element_type=F32))
        finish([(l, acc) for _, l, acc in state])


def _attn_b(qkv, scal, subln_g, out_scale, batch, seq_len, heads, ctx_k=None, ctx_v=None, tq=1024, ck=512):
    m = qkv.shape[0]
    hd = 2 * LANES
    has_ctx = ctx_k is not None
    tq = _tile(seq_len, tq)
    ck = _tile(seq_len, ck, *([ctx_k.shape[1]] if has_ctx else []))
    nb = seq_len // tq
    in_specs = [pl.BlockSpec(memory_space=pltpu.SMEM),
                pl.BlockSpec((tq, hd), lambda b, h, i: (b * nb + i, h)),
                pl.BlockSpec((seq_len, hd), lambda b, h, i: (b, heads + h)),
                pl.BlockSpec((seq_len, hd), lambda b, h, i: (b, 2 * heads + h))]
    args = [scal, qkv, qkv, qkv]
    if has_ctx:
        in_specs += [pl.BlockSpec((1, ctx_k.shape[1], hd), lambda b, h, i: (b, 0, h))] * 2
        args += [ctx_k, ctx_v]
    in_specs.append(pl.BlockSpec((1, hd), lambda b, h, i: (0, 0)))
    args.append(subln_g)
    return pl.pallas_call(
        functools.partial(_attn_b_kernel, has_ctx=has_ctx, out_scale=out_scale, ck=ck),
        grid=(batch, heads, nb),
        in_specs=in_specs,
        out_specs=pl.BlockSpec((tq, hd), lambda b, h, i: (b * nb + i, h)),
        out_shape=jax.ShapeDtypeStruct((m, heads * hd), BF16),
        compiler_params=_params("parallel", "arbitrary", "arbitrary"),
        name="attn_b_latent" if has_ctx else "attn_b_ctx",
    )(*args)


def _proj_residual_kernel(a_ref, w_ref, x_ref, gate_ref, o_ref):
    y = jnp.dot(a_ref[...], w_ref[...], preferred_element_type=F32)
    o_ref[...] = x_ref[...] + gate_ref[0] * y


def _proj_residual(a, w, layer, x, gate, seq_len, tm, tn):
    m, k = a.shape
    n = w.shape[2]
    per_batch = gate.shape[0] > 1
    tm = _tile(m, tm, seq_len) if per_batch else _tile(m, tm)
    tn = _tile(n, tn)
    mod_idx = _mod_index(per_batch, max(seq_len // tm, 1))
    gate_idx = lambda i, j: mod_idx(i, j)[:2] + (j,)
    return pl.pallas_call(
        _proj_residual_kernel,
        grid=(m // tm, n // tn),
        in_specs=[pl.BlockSpec((tm, k), lambda i, j: (i, 0)),
                  pl.BlockSpec((None, k, tn), lambda i, j: (layer, 0, j)),
                  pl.BlockSpec((tm, tn), lambda i, j: (i, j)),
                  pl.BlockSpec((1, 1, tn), gate_idx)],
        out_specs=pl.BlockSpec((tm, tn), lambda i, j: (i, j)),
        out_shape=jax.ShapeDtypeStruct((m, n), F32),
        compiler_params=_params("parallel", "arbitrary"),
        name="proj_residual",
    )(a, w, x, gate)


HALO = 16
FFN_COL_TILE = 512
QKV_COL_TILE = 1024


def _ffn_kernel(xp_ref, x_ref, xn_ref, g_ref, sh_ref, sc_ref, gate_ref, wg_ref, wv_ref, cw_ref, cb_ref, wd_ref,
                o_ref, h_ref, acc_ref, inv_ref, *, seq_len, tm):
    i = pl.program_id(0)
    f = pl.program_id(1)

    @pl.when(f == 0)
    def _():
        inv_ref[...] = lax.rsqrt(jnp.mean(jnp.square(x_ref[...]), axis=-1, keepdims=True) + EPS)

    @pl.when(f == 0)
    def _():
        g, sh, sc = g_ref[...], sh_ref[0], sc_ref[0]
        h_ref[0:HALO] = _adaln(xp_ref[...], g, sh, sc).astype(BF16)
        h_ref[HALO:HALO + tm] = (x_ref[...] * inv_ref[...] * (g * (1.0 + sc)) + sh).astype(BF16)
        h_ref[HALO + tm:] = _adaln(xn_ref[...], g, sh, sc).astype(BF16)
        acc_ref[...] = jnp.zeros_like(acc_ref)

    gate_pre = jnp.dot(h_ref[...], wg_ref[...], preferred_element_type=F32)
    val = jnp.dot(h_ref[HALO:HALO + tm], wv_ref[...], preferred_element_type=F32)
    pos = (i * tm + lax.broadcasted_iota(jnp.int32, (tm, 1), 0)) & (seq_len - 1)
    up = jnp.where(pos != 0, gate_pre[HALO - 1:HALO - 1 + tm], 0.0)
    dn = jnp.where(pos != seq_len - 1, gate_pre[HALO + 1:HALO + 1 + tm], 0.0)
    gc = up * cw_ref[0:1] + gate_pre[HALO:HALO + tm] * cw_ref[1:2] + dn * cw_ref[2:3] + cb_ref[...]
    act = (gc / (1.0 + jnp.exp(-gc)) * val).astype(BF16)
    acc_ref[...] += jnp.dot(act, wd_ref[...], preferred_element_type=F32)

    @pl.when(f == pl.num_programs(1) - 1)
    def _():
        o_ref[...] = x_ref[...] + gate_ref[0] * acc_ref[...]


def _ffn(x, g, shift, scale, gate, w_up, conv_w, conv_b, w_down, layer, seq_len, tm):
    m, d = x.shape
    ff = w_down.shape[1]
    per_batch = shift.shape[0] > 1
    tm = _tile(m, tm, seq_len) if per_batch else _tile(m, tm)
    tf = w_up.shape[3]
    assert (seq_len % tm == 0 or tm % seq_len == 0) and tm % HALO == 0
    assert seq_len & (seq_len - 1) == 0, "token position uses a power-of-two mask"
    mod_idx = _mod_index(per_batch, max(seq_len // tm, 1))
    hb = tm // HALO
    n_halo_blocks = m // HALO
    nf = ff // tf
    return pl.pallas_call(
        functools.partial(_ffn_kernel, seq_len=seq_len, tm=tm),
        grid=(m // tm, nf),
        in_specs=[pl.BlockSpec((HALO, d), lambda i, f: (jnp.maximum(i * hb - 1, 0), 0)),
                  pl.BlockSpec((tm, d), lambda i, f: (i, 0)),
                  pl.BlockSpec((HALO, d), lambda i, f: (jnp.minimum((i + 1) * hb, n_halo_blocks - 1), 0)),
                  pl.BlockSpec((1, d), lambda i, f: (0, 0)),
                  pl.BlockSpec((1, 1, d), mod_idx),
                  pl.BlockSpec((1, 1, d), mod_idx),
                  pl.BlockSpec((1, 1, d), mod_idx),
                  pl.BlockSpec((None, None, d, tf), lambda i, f: (layer, f, 0, 0)),
                  pl.BlockSpec((None, None, d, tf), lambda i, f: (layer, nf + f, 0, 0)),
                  pl.BlockSpec((3, tf), lambda i, f: (0, f)),
                  pl.BlockSpec((1, tf), lambda i, f: (0, f)),
                  pl.BlockSpec((None, tf, d), lambda i, f: (layer, f, 0))],
        out_specs=pl.BlockSpec((tm, d), lambda i, f: (i, 0)),
        out_shape=jax.ShapeDtypeStruct((m, d), F32),
        scratch_shapes=[pltpu.VMEM((tm + 2 * HALO, d), BF16), pltpu.VMEM((tm, d), F32), pltpu.VMEM((tm, 1), F32)],
        compiler_params=_params("parallel", "arbitrary"),
        name="conv_glu",
    )(x, x, x, g, shift, scale, gate, w_up, w_up, conv_w, conv_b, w_down)


def _rope_tables(n, dim):
    rows = n // GRID_W
    row = jnp.repeat(jnp.arange(rows, dtype=F32), GRID_W)
    col = jnp.tile(jnp.arange(GRID_W, dtype=F32), rows)
    half = dim // 2
    inv = ROPE_BASE ** (-jnp.arange(0, half, 2, dtype=F32) / half)
    ar, ac = row[:, None] * inv, col[:, None] * inv
    cos = jnp.concatenate([jnp.cos(ar), jnp.cos(ar), jnp.cos(ac), jnp.cos(ac)], axis=-1)
    sin = jnp.concatenate([-jnp.sin(ar), jnp.sin(ar), -jnp.sin(ac), jnp.sin(ac)], axis=-1)
    reps = MXU_COLS // dim
    return jnp.tile(cos, (1, reps)), jnp.tile(sin, (1, reps))


def kernel(x_prompt, x_sample, cache_a_k, cache_a_v, cache_b_k, cache_b_v, c, c_ctx, ada_w, ada_b, norm1_g, norm2_g, a_w_qkv, a_q_norm, a_k_norm, a_sink, a_w_o, b_w_qkv, b_q_norm, b_k_norm, b_lambda_q1, b_lambda_k1, b_lambda_q2, b_lambda_k2, b_subln, b_w_o, ffn_w_up, ffn_conv_w, ffn_conv_b, ffn_w_down):
    batch, seq, d = x_prompt.shape
    dec_batch, dec_seq, _ = x_sample.shape
    depth = ada_w.shape[0]
    a_heads = a_sink.shape[1]
    a_kvh, a_hd = cache_a_k.shape[3], cache_a_k.shape[4]
    b_heads, b_qk = cache_b_k.shape[3], cache_b_k.shape[5]
    b_vd = cache_b_v.shape[4]
    past = cache_a_k.shape[2]
    assert a_hd == LANES and b_qk == LANES and b_vd == 2 * LANES

    cond = jnp.concatenate([c, c_ctx[None, :], jnp.zeros((16 - dec_batch - 1, d), F32)], axis=0)
    mod = _modulation(cond, ada_w, ada_b).reshape(depth, 16, 6, d)
    cos, sin = _rope_tables(dec_seq, LANES)

    xp = x_prompt.reshape(batch * seq, d)
    xs = x_sample.reshape(dec_batch * dec_seq, d)
    a_k_list, a_v_list, b_k_list, b_v_list = [], [], [], []
    a_wo, b_wo, w_down = a_w_o.astype(BF16), b_w_o.astype(BF16), ffn_w_down.astype(BF16)
    a_wqkv = _chunk_columns(a_w_qkv, _tile(a_w_qkv.shape[2], QKV_COL_TILE))
    b_wqkv = _chunk_columns(b_w_qkv, _tile(b_w_qkv.shape[2], QKV_COL_TILE))
    w_up = _chunk_columns(ffn_w_up, _tile(ffn_w_down.shape[1], FFN_COL_TILE))

    for l in range(depth):
        mod_s = [mod[l, :dec_batch, t][:, None, :] for t in range(6)]
        mod_p = [mod[l, dec_batch:dec_batch + 1, t][:, None, :] for t in range(6)]
        n1 = norm1_g[l][None, :]
        n2 = norm2_g[l][None, :]
        j = l // 2
        if l % 2 == 0:
            w_qkv, w_o = a_wqkv, a_wo
            qn, kn = a_heads * a_hd, a_kvh * a_hd
            q_gain = a_q_norm[j] * (a_hd ** -0.5 * LOG2E)
            head_gain = jnp.concatenate([jnp.tile(q_gain, a_heads),
                                         jnp.tile(a_k_norm[j], a_kvh), jnp.ones((kn,), F32)])[None, :]
            sink_ok = (jnp.max(jnp.abs(a_sink[j])) * LOG2E <= SCORE_BOUND).astype(F32)
            sink_p = jnp.concatenate([a_sink[j], sink_ok * _scores_bounded(q_gain, a_k_norm[j], a_hd)[None]])[None, :]
            sink_s = jnp.concatenate([a_sink[j], sink_ok * _scores_bounded(q_gain, a_k_norm[j], a_hd,
                                                                          cache_a_k[:, j])[None]])[None, :]
            qkv_p = _qkv(xp, n1, mod_p[0], mod_p[1], w_qkv, j, head_gain, qn + kn, seq, None, F32, tm=512)
            op = _attn_a(qkv_p, sink_p, batch, seq, a_heads, a_kvh)
            qkv_s = _qkv(xs, n1, mod_s[0], mod_s[1], w_qkv, j, head_gain, qn + kn, dec_seq, (cos, sin), BF16,
                         tm=1024)
            ctx_k = cache_a_k[:, j].reshape(dec_batch, past, kn).astype(BF16)
            ctx_v = cache_a_v[:, j].reshape(dec_batch, past, kn).astype(BF16)
            os_ = _attn_a(qkv_s, sink_s, dec_batch, dec_seq, a_heads, a_kvh, ctx_k, ctx_v)
            a_k_list.append(qkv_p[:, qn:qn + kn].reshape(batch, seq, a_kvh, a_hd))
            a_v_list.append(qkv_p[:, qn + kn:].reshape(batch, seq, a_kvh, a_hd))
        else:
            w_qkv, w_o = b_wqkv, b_wo
            lambda_init = 0.8 - 0.6 * math.exp(-0.3 * l)
            lam = (jnp.exp(jnp.sum(b_lambda_q1[j] * b_lambda_k1[j])) - jnp.exp(jnp.sum(b_lambda_q2[j] * b_lambda_k2[j]))
                   + lambda_init)
            qn = b_heads * 2 * b_qk
            q_gain = b_q_norm[j] * (b_qk ** -0.5 * LOG2E)
            head_gain = jnp.concatenate([jnp.tile(q_gain, 2 * b_heads),
                                         jnp.tile(b_k_norm[j], 2 * b_heads), jnp.ones((qn,), F32)])[None, :]
            subln = b_subln[j][None, :]
            scal_p = jnp.stack([lam, _scores_bounded(q_gain, b_k_norm[j], b_qk)])[None, :]
            scal_s = jnp.stack([lam, _scores_bounded(q_gain, b_k_norm[j], b_qk, cache_b_k[:, j])])[None, :]
            qkv_p = _qkv(xp, n1, mod_p[0], mod_p[1], w_qkv, j, head_gain, 2 * qn, seq, None, F32, tm=512)
            op = _attn_b(qkv_p, scal_p, subln, 1.0 - lambda_init, batch, seq, b_heads)
            qkv_s = _qkv(xs, n1, mod_s[0], mod_s[1], w_qkv, j, head_gain, 2 * qn, dec_seq, (cos, sin), BF16,
                         tm=1024)
            ctx_k = cache_b_k[:, j].reshape(dec_batch, past, qn).astype(BF16)
            ctx_v = cache_b_v[:, j].reshape(dec_batch, past, qn).astype(BF16)
            os_ = _attn_b(qkv_s, scal_s, subln, 1.0 - lambda_init, dec_batch, dec_seq, b_heads, ctx_k, ctx_v)
            b_k_list.append(qkv_p[:, qn:2 * qn].reshape(batch, seq, b_heads, 2, b_qk))
            b_v_list.append(qkv_p[:, 2 * qn:].reshape(batch, seq, b_heads, b_vd))
        xp = _proj_residual(op, w_o, j, xp, mod_p[2], seq, tm=512, tn=2048)
        xs = _proj_residual(os_, w_o, j, xs, mod_s[2], dec_seq, tm=512, tn=2048)
        cb = ffn_conv_b[l][None, :]
        xp = _ffn(xp, n2, mod_p[3], mod_p[4], mod_p[5], w_up, ffn_conv_w[l], cb, w_down, l, seq, tm=512)
        xs = _ffn(xs, n2, mod_s[3], mod_s[4], mod_s[5], w_up, ffn_conv_w[l], cb, w_down, l, dec_seq, tm=512)

    return (xp.reshape(batch, seq, d), xs.reshape(dec_batch, dec_seq, d),
            jnp.stack(a_k_list, axis=1), jnp.stack(a_v_list, axis=1),
            jnp.stack(b_k_list, axis=1), jnp.stack(b_v_list, axis=1))
```

```python
import functools
import math

import jax
import jax.numpy as jnp
from jax import lax
from jax.experimental import pallas as pl
from jax.experimental.pallas import tpu as pltpu

GRID_W = 64
A_WINDOW = 128
ROPE_BASE = 10000.0
EPS = 1e-6
NEG_INF = -1e30
LANES = 128
MXU_COLS = 256
LOG2E = math.log2(math.e)
VMEM_LIMIT = 56 * 1024 * 1024

BF16 = jnp.bfloat16
F32 = jnp.float32


def _params(*sem):
    return pltpu.CompilerParams(dimension_semantics=sem, vmem_limit_bytes=VMEM_LIMIT)


def _tile(size, target, *also):
    t = min(target, size)
    while any(s % t for s in (size,) + also):
        t //= 2
    return t


def _rms(x):
    return x * lax.rsqrt(jnp.mean(x * x, axis=-1, keepdims=True) + EPS)


def _adaln(x, g, shift, scale):
    return _rms(x) * g * (1.0 + scale) + shift


def _mod_index(per_batch, tiles_per_seq):
    if per_batch:
        return lambda i, j: (i // tiles_per_seq, 0, 0)
    return lambda i, j: (0, 0, 0)


def _modulation_kernel(c_ref, w_ref, b_ref, o_ref):
    c = c_ref[...]
    s = (c / (1.0 + jnp.exp(-c))).astype(BF16)
    o_ref[0] = jnp.dot(s, w_ref[0].astype(BF16), preferred_element_type=F32) + b_ref[0]


def _modulation(cond, ada_w, ada_b, tn=1024):
    depth, d, n = ada_w.shape
    r = cond.shape[0]
    return pl.pallas_call(
        _modulation_kernel,
        grid=(depth, n // tn),
        in_specs=[pl.BlockSpec((r, d), lambda l, j: (0, 0)),
                  pl.BlockSpec((1, d, tn), lambda l, j: (l, 0, j)),
                  pl.BlockSpec((1, 1, tn), lambda l, j: (l, 0, j))],
        out_specs=pl.BlockSpec((1, r, tn), lambda l, j: (l, 0, j)),
        out_shape=jax.ShapeDtypeStruct((depth, r, n), F32),
        compiler_params=_params("arbitrary", "arbitrary"),
        name="modulation",
    )(cond, ada_w, ada_b.reshape(depth, 1, n))


def _head_matrices():
    i = jnp.arange(MXU_COLS)
    mean = jnp.where(i[:, None] // LANES == i[None, :] // LANES, 1.0 / LANES, 0.0)
    partner = jnp.where((i & 63) < 32, i + 32, i - 32)
    perm = (i[:, None] == partner[None, :])
    return mean.astype(BF16), perm.astype(BF16)


def _qkv_kernel(*refs, n_norm_cols, rope):
    if rope:
        x_ref, g_ref, sh_ref, sc_ref, w_ref, hg_ref, mean_ref, perm_ref, cos_ref, sin_ref, o_ref, h_ref, inv_ref = refs
    else:
        x_ref, g_ref, sh_ref, sc_ref, w_ref, hg_ref, mean_ref, o_ref, h_ref, inv_ref = refs
    j = pl.program_id(1)

    @pl.when(j == 0)
    def _():
        inv_ref[...] = lax.rsqrt(jnp.mean(jnp.square(x_ref[...]), axis=-1, keepdims=True) + EPS)

    @pl.when(j == 0)
    def _():
        gain = g_ref[...] * (1.0 + sc_ref[0])
        h_ref[...] = (x_ref[...] * inv_ref[...] * gain + sh_ref[0]).astype(BF16)

    tn = o_ref.shape[1]
    subs = tn // MXU_COLS
    y_all = jnp.dot(h_ref[...], w_ref[...], preferred_element_type=F32)

    def epilogue(n_normed):
        for c in range(subs):
            sl = slice(c * MXU_COLS, (c + 1) * MXU_COLS)
            y = y_all[:, sl]
            if c < n_normed:
                ms = jnp.dot((y * y).astype(BF16), mean_ref[...], preferred_element_type=F32)
                z = y * hg_ref[:, sl]
                if rope:
                    partner = jnp.dot(z.astype(BF16), perm_ref[...], preferred_element_type=F32)
                    z = z * cos_ref[...] + partner * sin_ref[...]
                y = z * lax.rsqrt(ms + EPS)
            o_ref[:, sl] = y.astype(o_ref.dtype)

    full_tiles, part_subs = n_norm_cols // tn, (n_norm_cols % tn) // MXU_COLS
    pl.when(j < full_tiles)(functools.partial(epilogue, subs))
    if part_subs:
        pl.when(j == full_tiles)(functools.partial(epilogue, part_subs))
    pl.when(j >= full_tiles + (1 if part_subs else 0))(functools.partial(epilogue, 0))


def _qkv(x, g, shift, scale, w, layer, head_gain, n_norm_cols, seq_len, rope_tabs, out_dtype, tm, tn):
    m, d = x.shape
    n = w.shape[2]
    per_batch = shift.shape[0] > 1
    tn = _tile(n, tn)
    assert tn % MXU_COLS == 0 and n_norm_cols % MXU_COLS == 0
    tm = _tile(m, tm, seq_len) if (per_batch or rope_tabs is not None) else _tile(m, tm)
    tiles_per_seq = max(seq_len // tm, 1)
    mod_idx = _mod_index(per_batch, tiles_per_seq)
    mean_mat, perm_mat = _head_matrices()
    const_spec = pl.BlockSpec((MXU_COLS, MXU_COLS), lambda i, j: (0, 0))
    in_specs = [pl.BlockSpec((tm, d), lambda i, j: (i, 0)),
                pl.BlockSpec((1, d), lambda i, j: (0, 0)),
                pl.BlockSpec((1, 1, d), mod_idx),
                pl.BlockSpec((1, 1, d), mod_idx),
                pl.BlockSpec((None, d, tn), lambda i, j: (layer, 0, j)),
                pl.BlockSpec((1, tn), lambda i, j: (0, j)),
                const_spec]
    args = [x, g, shift, scale, w, head_gain, mean_mat]
    if rope_tabs is not None:
        in_specs += [const_spec] + [pl.BlockSpec((tm, MXU_COLS), lambda i, j: (i % tiles_per_seq, 0))] * 2
        args += [perm_mat] + list(rope_tabs)
    return pl.pallas_call(
        functools.partial(_qkv_kernel, n_norm_cols=n_norm_cols, rope=rope_tabs is not None),
        grid=(m // tm, n // tn),
        in_specs=in_specs,
        out_specs=pl.BlockSpec((tm, tn), lambda i, j: (i, j)),
        out_shape=jax.ShapeDtypeStruct((m, n), out_dtype),
        scratch_shapes=[pltpu.VMEM((tm, d), BF16), pltpu.VMEM((tm, 1), F32)],
        compiler_params=_params("parallel", "arbitrary"),
        name="adaln_qkv",
    )(*args)


def _attn_a_kernel(*refs, groups, window, tq, n_blocks, kv_per_step):
    if window:
        (sink_ref, q_ref, kp_ref, km_ref, kn_ref, kc_ref, vp_ref, vm_ref, vn_ref, vc_ref, o_ref) = refs
    else:
        sink_ref, q_ref, km_ref, vm_ref, o_ref = refs
    kvh0 = pl.program_id(2) * kv_per_step
    if window:
        i = pl.program_id(1)
        keys = [jnp.concatenate([kp_ref[...], km_ref[...], kn_ref[...], kc_ref[0]], axis=0)]
        vals = [jnp.concatenate([vp_ref[...], vm_ref[...], vn_ref[...], vc_ref[0]], axis=0)]
        n_lat = tq + 2 * A_WINDOW
        qi = lax.broadcasted_iota(jnp.int32, (tq, keys[0].shape[0]), 0)
        kj = lax.broadcasted_iota(jnp.int32, (tq, keys[0].shape[0]), 1)
        in_seq = ((i > 0) | (kj >= A_WINDOW)) & ((i < n_blocks - 1) | (kj < tq + A_WINDOW))
        mask = (kj >= n_lat) | ((jnp.abs(kj - A_WINDOW - qi) <= A_WINDOW) & in_seq)
    else:
        keys = [km_ref[:, kk * LANES:(kk + 1) * LANES].astype(BF16) for kk in range(kv_per_step)]
        vals = [vm_ref[:, kk * LANES:(kk + 1) * LANES].astype(BF16) for kk in range(kv_per_step)]
    n_heads = sink_ref.shape[1] - 1

    def heads(bounded):
        for kk in range(kv_per_step):
            for g in range(groups):
                sl = slice((kk * groups + g) * LANES, (kk * groups + g + 1) * LANES)
                q = q_ref[:, sl].astype(BF16)
                s = lax.dot_general(q, keys[kk], (((1,), (1,)), ((), ())), preferred_element_type=F32)
                if window:
                    s = jnp.where(mask, s, NEG_INF)
                sink = sink_ref[0, (kvh0 + kk) * groups + g] * LOG2E
                if bounded:
                    p = jnp.exp2(s)
                    denom = jnp.sum(p, axis=-1, keepdims=True) + jnp.exp2(jnp.full((1, 1), sink, F32))
                else:
                    m = jnp.maximum(jnp.max(s, axis=-1, keepdims=True), sink)
                    p = jnp.exp2(s - m)
                    denom = jnp.sum(p, axis=-1, keepdims=True) + jnp.exp2(sink - m)
                o = jnp.dot(p.astype(BF16), vals[kk], preferred_element_type=F32)
                o_ref[:, sl] = (o / denom).astype(o_ref.dtype)

    pl.when(sink_ref[0, n_heads] != 0.0)(functools.partial(heads, True))
    pl.when(sink_ref[0, n_heads] == 0.0)(functools.partial(heads, False))


def _attn_a(qkv, sink, batch, seq_len, heads, kv_heads, ctx_k=None, ctx_v=None, tq=256):
    m = qkv.shape[0]
    groups = heads // kv_heads
    window = ctx_k is not None
    tq = _tile(seq_len, tq)
    nb = seq_len // tq
    kps = 1 if window else kv_heads
    assert heads % kps == 0
    k_col, v_col = heads // kps, (heads + kv_heads) // kps
    r = tq // A_WINDOW
    rows128 = seq_len // A_WINDOW

    def main_rows(b, i):
        return b * nb + i

    def prev_rows(b, i):
        return b * rows128 + jnp.maximum(i * r - 1, 0)

    def next_rows(b, i):
        return b * rows128 + jnp.minimum((i + 1) * r, rows128 - 1)

    sink_spec = pl.BlockSpec(memory_space=pltpu.SMEM)
    q_spec = pl.BlockSpec((tq, kps * groups * LANES), lambda b, i, k: (main_rows(b, i), k))

    def kv_specs(col):
        main = pl.BlockSpec((tq, kps * LANES), lambda b, i, k: (main_rows(b, i), col + k))
        if not window:
            return [main]
        return [pl.BlockSpec((A_WINDOW, LANES), lambda b, i, k: (prev_rows(b, i), col + k)),
                main,
                pl.BlockSpec((A_WINDOW, LANES), lambda b, i, k: (next_rows(b, i), col + k)),
                pl.BlockSpec((1, ctx_k.shape[1], LANES), lambda b, i, k: (b, 0, k))]

    if window:
        args = [sink, qkv, qkv, qkv, qkv, ctx_k, qkv, qkv, qkv, ctx_v]
    else:
        args = [sink, qkv, qkv, qkv]
    return pl.pallas_call(
        functools.partial(_attn_a_kernel, groups=groups, window=window, tq=tq, n_blocks=nb, kv_per_step=kps),
        grid=(batch, nb, kv_heads // kps),
        in_specs=[sink_spec, q_spec] + kv_specs(k_col) + kv_specs(v_col),
        out_specs=pl.BlockSpec((tq, kps * groups * LANES), lambda b, i, k: (main_rows(b, i), k)),
        out_shape=jax.ShapeDtypeStruct((m, heads * LANES), BF16),
        compiler_params=_params("parallel", "arbitrary", "arbitrary"),
        name="attn_a_window" if window else "attn_a_ctx",
    )(*args)


SCORE_BOUND = 60.0


def _scores_bounded(q_gain, k_gain, head_dim, ctx_k=None):
    q_norm = math.sqrt(head_dim) * jnp.max(jnp.abs(q_gain))
    k_norm = math.sqrt(head_dim) * jnp.max(jnp.abs(k_gain))
    if ctx_k is not None:
        k_norm = jnp.maximum(k_norm, jnp.sqrt(jnp.max(jnp.sum(jnp.square(ctx_k.astype(F32)), axis=-1))))
    return (q_norm * k_norm <= SCORE_BOUND).astype(F32)


def _lane_group_sum(p):
    return functools.reduce(jnp.add, [p[:, t * LANES:(t + 1) * LANES] for t in range(p.shape[1] // LANES)])


def _attn_b_kernel(*refs, has_ctx, out_scale, ck):
    if has_ctx:
        scal_ref, q_ref, k_ref, v_ref, kc_ref, vc_ref, sg_ref, o_ref = refs
    else:
        scal_ref, q_ref, k_ref, v_ref, sg_ref, o_ref = refs
    chunks =[(k_ref, v_ref, s0) for s0 in range(0, k_ref.shape[0], ck)]
    if has_ctx:
        chunks += [(kc_ref.at[0], vc_ref.at[0], s0) for s0 in range(0, kc_ref.shape[1], ck)]

    hd = 2 * LANES
    n_heads = o_ref.shape[1] // hd

    def scores(col, kr, s0):
        q = q_ref[:, col:col + LANES].astype(BF16)
        k = kr[s0:s0 + ck, col:col + LANES].astype(BF16)
        return lax.dot_general(q, k, (((1,), (1,)), ((), ())), preferred_element_type=F32)

    def finish(base, state):
        outs = [acc / jnp.sum(l, axis=-1, keepdims=True) for l, acc in state]
        o = outs[0] - scal_ref[0, 0] * outs[1]
        o_ref[:, base:base + hd] = (_rms(o) * sg_ref[...] * out_scale).astype(o_ref.dtype)

    def bounded_head(base):
        state = [None, None]
        for kr, vr, s0 in chunks:
            v = vr[s0:s0 + ck, base:base + hd].astype(BF16)
            for c in range(2):
                p = jnp.exp2(scores(base + c * LANES, kr, s0))
                l, acc = _lane_group_sum(p), jnp.dot(p.astype(BF16), v, preferred_element_type=F32)
                state[c] = (l, acc) if state[c] is None else (state[c][0] + l, state[c][1] + acc)
        finish(base, state)

    def online_head(base):
        state = [None, None]
        for kr, vr, s0 in chunks:
            v = vr[s0:s0 + ck, base:base + hd].astype(BF16)
            for c in range(2):
                s = scores(base + c * LANES, kr, s0)
                row_max = jnp.max(s, axis=-1, keepdims=True)
                if state[c] is None:
                    p = jnp.exp2(s - row_max)
                    state[c] = (row_max, _lane_group_sum(p), jnp.dot(p.astype(BF16), v, preferred_element_type=F32))
                else:
                    m_old, l_old, acc_old = state[c]
                    m_new = jnp.maximum(m_old, row_max)
                    alpha = jnp.exp2(m_old - m_new)
                    p = jnp.exp2(s - m_new)
                    state[c] = (m_new, alpha * l_old + _lane_group_sum(p),
                                alpha * acc_old + jnp.dot(p.astype(BF16), v, preferred_element_type=F32))
        finish(base, [(l, acc) for _, l, acc in state])

    @pl.when(scal_ref[0, 1] != 0.0)
    def _():
        for hh in range(n_heads):
            bounded_head(hh * hd)

    @pl.when(scal_ref[0, 1] == 0.0)
    def _():
        for hh in range(n_heads):
            online_head(hh * hd)


def _attn_b(qkv, scal, subln_g, out_scale, batch, seq_len, heads, ctx_k=None, ctx_v=None, tq=1024, ck=512):
    m = qkv.shape[0]
    hd = 2 * LANES
    has_ctx = ctx_k is not None
    tq = _tile(seq_len, tq)
    ck = _tile(seq_len, ck, *([ctx_k.shape[1]] if has_ctx else []))
    nb = seq_len // tq
    hps = 1 if has_ctx else heads
    wd = hps * hd
    n_hblk = heads // hps
    in_specs = [pl.BlockSpec(memory_space=pltpu.SMEM),
                pl.BlockSpec((tq, wd), lambda b, h, i: (b * nb + i, h)),
                pl.BlockSpec((seq_len, wd), lambda b, h, i: (b, n_hblk + h)),
                pl.BlockSpec((seq_len, wd), lambda b, h, i: (b, 2 * n_hblk + h))]
    args = [scal, qkv, qkv, qkv]
    if has_ctx:
        in_specs += [pl.BlockSpec((1, ctx_k.shape[1], hd), lambda b, h, i: (b, 0, h))] * 2
        args += [ctx_k, ctx_v]
    in_specs.append(pl.BlockSpec((1, hd), lambda b, h, i: (0, 0)))
    args.append(subln_g)
    return pl.pallas_call(
        functools.partial(_attn_b_kernel, has_ctx=has_ctx, out_scale=out_scale, ck=ck),
        grid=(batch, n_hblk, nb),
        in_specs=in_specs,
        out_specs=pl.BlockSpec((tq, wd), lambda b, h, i: (b * nb + i, h)),
        out_shape=jax.ShapeDtypeStruct((m, heads * hd), BF16),
        compiler_params=_params("parallel", "arbitrary", "arbitrary"),
        name="attn_b_latent" if has_ctx else "attn_b_ctx",
    )(*args)


def _proj_residual_kernel(a_ref, w_ref, x_ref, gate_ref, o_ref):
    y = jnp.dot(a_ref[...], w_ref[...], preferred_element_type=F32)
    o_ref[...] = x_ref[...] + gate_ref[0] * y


def _proj_residual(a, w, layer, x, gate, seq_len, tm, tn):
    m, k = a.shape
    n = w.shape[2]
    per_batch = gate.shape[0] > 1
    tm = _tile(m, tm, seq_len) if per_batch else _tile(m, tm)
    tn = _tile(n, tn)
    mod_idx = _mod_index(per_batch, max(seq_len // tm, 1))
    gate_idx = lambda i, j: mod_idx(i, j)[:2] + (j,)
    return pl.pallas_call(
        _proj_residual_kernel,
        grid=(m // tm, n // tn),
        in_specs=[pl.BlockSpec((tm, k), lambda i, j: (i, 0)),
                  pl.BlockSpec((None, k, tn), lambda i, j: (layer, 0, j)),
                  pl.BlockSpec((tm, tn), lambda i, j: (i, j)),
                  pl.BlockSpec((1, 1, tn), gate_idx)],
        out_specs=pl.BlockSpec((tm, tn), lambda i, j: (i, j)),
        out_shape=jax.ShapeDtypeStruct((m, n), F32),
        compiler_params=_params("parallel", "arbitrary"),
        name="proj_residual",
    )(a, w, x, gate)


HALO = 16


def _ffn_kernel(xp_ref, x_ref, xn_ref, g_ref, sh_ref, sc_ref, gate_ref, wg_ref, wv_ref, cw_ref, cb_ref, wd_ref,
                o_ref, h_ref, inv_ref, *, seq_len, tm):
    i = pl.program_id(0)
    f = pl.program_id(1)

    @pl.when(f == 0)
    def _():
        inv_ref[...] = lax.rsqrt(jnp.mean(jnp.square(x_ref[...]), axis=-1, keepdims=True) + EPS)

    @pl.when(f == 0)
    def _():
        g, sh, sc = g_ref[...], sh_ref[0], sc_ref[0]
        h_ref[0:HALO] = _adaln(xp_ref[...], g, sh, sc).astype(BF16)
        h_ref[HALO:HALO + tm] = (x_ref[...] * inv_ref[...] * (g * (1.0 + sc)) + sh).astype(BF16)
        h_ref[HALO + tm:] = _adaln(xn_ref[...], g, sh, sc).astype(BF16)
        o_ref[...] = jnp.zeros_like(o_ref)

    gate_pre = jnp.dot(h_ref[...], wg_ref[...], preferred_element_type=F32)
    val = jnp.dot(h_ref[HALO:HALO + tm], wv_ref[...], preferred_element_type=F32)
    pos = (i * tm + lax.broadcasted_iota(jnp.int32, (tm, 1), 0)) & (seq_len - 1)
    up = jnp.where(pos != 0, gate_pre[HALO - 1:HALO - 1 + tm], 0.0)
    dn = jnp.where(pos != seq_len - 1, gate_pre[HALO + 1:HALO + 1 + tm], 0.0)
    gc = up * cw_ref[0:1] + gate_pre[HALO:HALO + tm] * cw_ref[1:2] + dn * cw_ref[2:3] + cb_ref[...]
    act = (gc / (1.0 + jnp.exp(-gc)) * val).astype(BF16)
    o_ref[...] += jnp.dot(act, wd_ref[...], preferred_element_type=F32)

    @pl.when(f == pl.num_programs(1) - 1)
    def _():
        o_ref[...] = x_ref[...] + gate_ref[0] * o_ref[...]


def _ffn(x, g, shift, scale, gate, w_up, conv_w, conv_b, w_down, layer, seq_len, tm, tf):
    m, d = x.shape
    ff = w_down.shape[1]
    per_batch = shift.shape[0] > 1
    tm = _tile(m, tm, seq_len) if per_batch else _tile(m, tm)
    tf = _tile(ff, tf)
    assert (seq_len % tm == 0 or tm % seq_len == 0) and tm % HALO == 0
    assert seq_len & (seq_len - 1) == 0, "token position uses a power-of-two mask"
    mod_idx = _mod_index(per_batch, max(seq_len // tm, 1))
    hb = tm // HALO
    n_halo_blocks = m // HALO
    nf = ff // tf
    return pl.pallas_call(
        functools.partial(_ffn_kernel, seq_len=seq_len, tm=tm),
        grid=(m // tm, nf),
        in_specs=[pl.BlockSpec((HALO, d), lambda i, f: (jnp.maximum(i * hb - 1, 0), 0)),
                  pl.BlockSpec((tm, d), lambda i, f: (i, 0)),
                  pl.BlockSpec((HALO, d), lambda i, f: (jnp.minimum((i + 1) * hb, n_halo_blocks - 1), 0)),
                  pl.BlockSpec((1, d), lambda i, f: (0, 0)),
                  pl.BlockSpec((1, 1, d), mod_idx),
                  pl.BlockSpec((1, 1, d), mod_idx),
                  pl.BlockSpec((1, 1, d), mod_idx),
                  pl.BlockSpec((None, d, tf), lambda i, f: (layer, 0, f)),
                  pl.BlockSpec((None, d, tf), lambda i, f: (layer, 0, nf + f)),
                  pl.BlockSpec((3, tf), lambda i, f: (0, f)),
                  pl.BlockSpec((1, tf), lambda i, f: (0, f)),
                  pl.BlockSpec((None, tf, d), lambda i, f: (layer, f, 0))],
        out_specs=pl.BlockSpec((tm, d), lambda i, f: (i, 0)),
        out_shape=jax.ShapeDtypeStruct((m, d), F32),
        scratch_shapes=[pltpu.VMEM((tm + 2 * HALO, d), BF16), pltpu.VMEM((tm, 1), F32)],
        compiler_params=_params("parallel", "arbitrary"),
        name="conv_glu",
    )(x, x, x, g, shift, scale, gate, w_up, w_up, conv_w, conv_b, w_down)


def _rope_tables(n, dim):
    rows = n // GRID_W
    row = jnp.repeat(jnp.arange(rows, dtype=F32), GRID_W)
    col = jnp.tile(jnp.arange(GRID_W, dtype=F32), rows)
    half = dim // 2
    inv = ROPE_BASE ** (-jnp.arange(0, half, 2, dtype=F32) / half)
    ar, ac = row[:, None] * inv, col[:, None] * inv
    cos = jnp.concatenate([jnp.cos(ar), jnp.cos(ar), jnp.cos(ac), jnp.cos(ac)], axis=-1)
    sin = jnp.concatenate([-jnp.sin(ar), jnp.sin(ar), -jnp.sin(ac), jnp.sin(ac)], axis=-1)
    reps = MXU_COLS // dim
    return jnp.tile(cos, (1, reps)), jnp.tile(sin, (1, reps))


def kernel(x_prompt, x_sample, cache_a_k, cache_a_v, cache_b_k, cache_b_v, c, c_ctx, ada_w, ada_b, norm1_g, norm2_g, a_w_qkv, a_q_norm, a_k_norm, a_sink, a_w_o, b_w_qkv, b_q_norm, b_k_norm, b_lambda_q1, b_lambda_k1, b_lambda_q2, b_lambda_k2, b_subln, b_w_o, ffn_w_up, ffn_conv_w, ffn_conv_b, ffn_w_down):
    batch, seq, d = x_prompt.shape
    dec_batch, dec_seq, _ = x_sample.shape
    depth = ada_w.shape[0]
    a_heads = a_sink.shape[1]
    a_kvh, a_hd = cache_a_k.shape[3], cache_a_k.shape[4]
    b_heads, b_qk = cache_b_k.shape[3], cache_b_k.shape[5]
    b_vd = cache_b_v.shape[4]
    past = cache_a_k.shape[2]
    assert a_hd == LANES and b_qk == LANES and b_vd == 2 * LANES

    cond = jnp.concatenate([c, c_ctx[None, :], jnp.zeros((16 - dec_batch - 1, d), F32)], axis=0)
    mod = _modulation(cond, ada_w, ada_b).reshape(depth, 16, 6, d)
    cos, sin = _rope_tables(dec_seq, LANES)

    xp = x_prompt.reshape(batch * seq, d)
    xs = x_sample.reshape(dec_batch * dec_seq, d)
    a_k_list, a_v_list, b_k_list, b_v_list = [], [], [], []
    a_wqkv, a_wo, b_wqkv, b_wo, w_up, w_down = [w.astype(BF16) for w in
                                               (a_w_qkv, a_w_o, b_w_qkv, b_w_o, ffn_w_up, ffn_w_down)]

    for l in range(depth):
        mod_s = [mod[l, :dec_batch, t][:, None, :] for t in range(6)]
        mod_p = [mod[l, dec_batch:dec_batch + 1, t][:, None, :] for t in range(6)]
        n1 = norm1_g[l][None, :]
        n2 = norm2_g[l][None, :]
        j = l // 2
        if l % 2 == 0:
            w_qkv, w_o = a_wqkv, a_wo
            qn, kn = a_heads * a_hd, a_kvh * a_hd
            q_gain = a_q_norm[j] * (a_hd ** -0.5 * LOG2E)
            head_gain = jnp.concatenate([jnp.tile(q_gain, a_heads),
                                         jnp.tile(a_k_norm[j], a_kvh), jnp.ones((kn,), F32)])[None, :]
            sink_ok = (jnp.max(jnp.abs(a_sink[j])) * LOG2E <= SCORE_BOUND).astype(F32)
            sink_p = jnp.concatenate([a_sink[j], sink_ok * _scores_bounded(q_gain, a_k_norm[j], a_hd)[None]])[None, :]
            sink_s = jnp.concatenate([a_sink[j], sink_ok * _scores_bounded(q_gain, a_k_norm[j], a_hd,
                                                                          cache_a_k[:, j])[None]])[None, :]
            qkv_p = _qkv(xp, n1, mod_p[0], mod_p[1], w_qkv, j, head_gain, qn + kn, seq, None, F32, tm=512, tn=1024)
            op = _attn_a(qkv_p, sink_p, batch, seq, a_heads, a_kvh)
            qkv_s = _qkv(xs, n1, mod_s[0], mod_s[1], w_qkv, j, head_gain, qn + kn, dec_seq, (cos, sin), BF16,
                         tm=1024, tn=1024)
            ctx_k = cache_a_k[:, j].reshape(dec_batch, past, kn).astype(BF16)
            ctx_v = cache_a_v[:, j].reshape(dec_batch, past, kn).astype(BF16)
            os_ = _attn_a(qkv_s, sink_s, dec_batch, dec_seq, a_heads, a_kvh, ctx_k, ctx_v)
            a_k_list.append(qkv_p[:, qn:qn + kn].reshape(batch, seq, a_kvh, a_hd))
            a_v_list.append(qkv_p[:, qn + kn:].reshape(batch, seq, a_kvh, a_hd))
        else:
            w_qkv, w_o = b_wqkv, b_wo
            lambda_init = 0.8 - 0.6 * math.exp(-0.3 * l)
            lam = (jnp.exp(jnp.sum(b_lambda_q1[j] * b_lambda_k1[j])) - jnp.exp(jnp.sum(b_lambda_q2[j] * b_lambda_k2[j]))
                   + lambda_init)
            qn = b_heads * 2 * b_qk
            q_gain = b_q_norm[j] * (b_qk ** -0.5 * LOG2E)
            head_gain = jnp.concatenate([jnp.tile(q_gain, 2 * b_heads),
                                         jnp.tile(b_k_norm[j], 2 * b_heads), jnp.ones((qn,), F32)])[None, :]
            subln = b_subln[j][None, :]
            scal_p = jnp.stack([lam, _scores_bounded(q_gain, b_k_norm[j], b_qk)])[None, :]
            scal_s = jnp.stack([lam, _scores_bounded(q_gain, b_k_norm[j], b_qk, cache_b_k[:, j])])[None, :]
            qkv_p = _qkv(xp, n1, mod_p[0], mod_p[1], w_qkv, j, head_gain, 2 * qn, seq, None, F32, tm=512, tn=1024)
            op = _attn_b(qkv_p, scal_p, subln, 1.0 - lambda_init, batch, seq, b_heads)
            qkv_s = _qkv(xs, n1, mod_s[0], mod_s[1], w_qkv, j, head_gain, 2 * qn, dec_seq, (cos, sin), BF16,
                         tm=1024, tn=1024)
            ctx_k = cache_b_k[:, j].reshape(dec_batch, past, qn).astype(BF16)
            ctx_v = cache_b_v[:, j].reshape(dec_batch, past, qn).astype(BF16)
            os_ = _attn_b(qkv_s, scal_s, subln, 1.0 - lambda_init, dec_batch, dec_seq, b_heads, ctx_k, ctx_v)
            b_k_list.append(qkv_p[:, qn:2 * qn].reshape(batch, seq, b_heads, 2, b_qk))
            b_v_list.append(qkv_p[:, 2 * qn:].reshape(batch, seq, b_heads, b_vd))
        xp = _proj_residual(op, w_o, j, xp, mod_p[2], seq, tm=512, tn=2048)
        xs = _proj_residual(os_, w_o, j, xs, mod_s[2], dec_seq, tm=512, tn=2048)
        cb = ffn_conv_b[l][None, :]
        xp = _ffn(xp, n2, mod_p[3], mod_p[4], mod_p[5], w_up, ffn_conv_w[l], cb, w_down, l, seq, tm=1024, tf=512)
        xs = _ffn(xs, n2, mod_s[3], mod_s[4], mod_s[5], w_up, ffn_conv_w[l], cb, w_down, l, dec_seq, tm=1024, tf=512)

    return (xp.reshape(batch, seq, d), xs.reshape(dec_batch, dec_seq, d),
            jnp.stack(a_k_list, axis=1), jnp.stack(a_v_list, axis=1),
            jnp.stack(b_k_list, axis=1), jnp.stack(b_v_list, axis=1))
```

```python
import functools
import math
from typing import NamedTuple, Optional

import jax
import jax.numpy as jnp
from jax import lax
from jax.experimental import pallas as pl
from jax.experimental.pallas import tpu as pltpu

GRID_W = 64
A_WINDOW = 128
ROPE_BASE = 10000.0
EPS = 1e-6
NEG_INF = -1e30
LANES = 128
MXU_COLS = 256
LOG2E = math.log2(math.e)
VMEM_LIMIT = 56 * 1024 * 1024

BF16 = jnp.bfloat16
F32 = jnp.float32


def _params(*sem):
    return pltpu.CompilerParams(dimension_semantics=sem, vmem_limit_bytes=VMEM_LIMIT)


def _tile(size, target, *also):
    t = min(target, size)
    while any(s % t for s in (size,) + also):
        t //= 2
    return t


def _rms(x):
    return x * lax.rsqrt(jnp.mean(x * x, axis=-1, keepdims=True) + EPS)


def _adaln(x, g, shift, scale):
    return _rms(x) * g * (1.0 + scale) + shift


def _mod_index(per_batch, tiles_per_seq):
    if per_batch:
        return lambda i, j: (i // tiles_per_seq, 0, 0)
    return lambda i, j: (0, 0, 0)


def _modulation_kernel(c_ref, w_ref, b_ref, o_ref):
    c = c_ref[...]
    s = (c / (1.0 + jnp.exp(-c))).astype(BF16)
    o_ref[0] = jnp.dot(s, w_ref[0].astype(BF16), preferred_element_type=F32) + b_ref[0]


def _modulation(cond, ada_w, ada_b, tn=1024):
    depth, d, n = ada_w.shape
    r = cond.shape[0]
    return pl.pallas_call(
        _modulation_kernel,
        grid=(depth, n // tn),
        in_specs=[pl.BlockSpec((r, d), lambda l, j: (0, 0)),
                  pl.BlockSpec((1, d, tn), lambda l, j: (l, 0, j)),
                  pl.BlockSpec((1, 1, tn), lambda l, j: (l, 0, j))],
        out_specs=pl.BlockSpec((1, r, tn), lambda l, j: (l, 0, j)),
        out_shape=jax.ShapeDtypeStruct((depth, r, n), F32),
        compiler_params=_params("arbitrary", "arbitrary"),
        name="modulation",
    )(cond, ada_w, ada_b.reshape(depth, 1, n))


def _head_matrices():
    i = jnp.arange(MXU_COLS)
    mean = jnp.where(i[:, None] // LANES == i[None, :] // LANES, 1.0 / LANES, 0.0)
    partner = jnp.where((i & 63) < 32, i + 32, i - 32)
    perm = (i[:, None] == partner[None, :])
    return mean.astype(BF16), perm.astype(BF16)


class _KvState(NamedTuple):
    k_prev: Optional[jax.Array]
    v_prev: Optional[jax.Array]
    slots: int
    slot: int
    k_start: int
    k_width: int
    v_start: int
    v_width: int


def _qkv_kernel(*refs, n_norm_cols, n_tiles, rope, state, n_aliased):
    n_in = 7 + (3 if rope else 0) + n_aliased
    x_ref, g_ref, sh_ref, sc_ref, w_ref, hg_ref, mean_ref = refs[:7]
    if rope:
        perm_ref, cos_ref, sin_ref = refs[7:10]
    if state is not None:
        o_ref, ks_ref, vs_ref, h_ref, inv_ref = refs[n_in:]
    else:
        o_ref, h_ref, inv_ref = refs[n_in:]
    j = pl.program_id(1)

    @pl.when(j == 0)
    def _():
        inv_ref[...] = lax.rsqrt(jnp.mean(jnp.square(x_ref[...]), axis=-1, keepdims=True) + EPS)

    @pl.when(j == 0)
    def _():
        gain = g_ref[...] * (1.0 + sc_ref[0])
        h_ref[...] = (x_ref[...] * inv_ref[...] * gain + sh_ref[0]).astype(BF16)

    tn = o_ref.shape[1]
    subs = tn // MXU_COLS
    y_all = jnp.dot(h_ref[...], w_ref[...], preferred_element_type=F32)

    def epilogue(tile):
        for c in range(subs):
            col = tile * tn + c * MXU_COLS
            sl = slice(c * MXU_COLS, (c + 1) * MXU_COLS)
            y = y_all[:, sl]
            if col < n_norm_cols:
                ms = jnp.dot((y * y).astype(BF16), mean_ref[...], preferred_element_type=F32)
                z = y * hg_ref[:, sl]
                if rope:
                    partner = jnp.dot(z.astype(BF16), perm_ref[...], preferred_element_type=F32)
                    z = z * cos_ref[...] + partner * sin_ref[...]
                y = z * lax.rsqrt(ms + EPS)
            o_ref[:, sl] = y.astype(o_ref.dtype)
            if state is not None:
                for ref, start, width in ((ks_ref, state.k_start, state.k_width), (vs_ref, state.v_start, state.v_width)):
                    if start <= col < start + width:
                        off = (col - start) % ref.shape[2]
                        ref[:, :, off:off + MXU_COLS] = y.reshape(ref.shape[0], ref.shape[1], MXU_COLS)

    for tile in range(n_tiles):
        pl.when(j == tile)(functools.partial(epilogue, tile))


def _qkv(x, g, shift, scale, w, layer, head_gain, n_norm_cols, seq_len, rope_tabs, out_dtype, tm, tn, state=None):
    m, d = x.shape
    n = w.shape[2]
    per_batch = shift.shape[0] > 1
    tn = _tile(n, tn)
    assert tn % MXU_COLS == 0 and n_norm_cols % MXU_COLS == 0
    tm = _tile(m, tm, seq_len) if (per_batch or rope_tabs is not None) else _tile(m, tm)
    tiles_per_seq = max(seq_len // tm, 1)
    mod_idx = _mod_index(per_batch, tiles_per_seq)
    mean_mat, perm_mat = _head_matrices()
    const_spec = pl.BlockSpec((MXU_COLS, MXU_COLS), lambda i, j: (0, 0))
    in_specs = [pl.BlockSpec((tm, d), lambda i, j: (i, 0)),
                pl.BlockSpec((1, d), lambda i, j: (0, 0)),
                pl.BlockSpec((1, 1, d), mod_idx),
                pl.BlockSpec((1, 1, d), mod_idx),
                pl.BlockSpec((None, d, tn), lambda i, j: (layer, 0, j)),
                pl.BlockSpec((1, tn), lambda i, j: (0, j)),
                const_spec]
    args = [x, g, shift, scale, w, head_gain, mean_mat]
    if rope_tabs is not None:
        in_specs += [const_spec] + [pl.BlockSpec((tm, MXU_COLS), lambda i, j: (i % tiles_per_seq, 0))] * 2
        args += [perm_mat] + list(rope_tabs)
    out_specs = [pl.BlockSpec((tm, tn), lambda i, j: (i, j))]
    out_shape = [jax.ShapeDtypeStruct((m, n), out_dtype)]
    aliases = {}
    if state is not None:
        assert tm % seq_len == 0 and not per_batch
        seqs = tm // seq_len
        for start, width, prev in ((state.k_start, state.k_width, state.k_prev),
                                   (state.v_start, state.v_width, state.v_prev)):
            bw = min(tn, width)
            assert width % bw == 0 and start % MXU_COLS == 0 and (start % tn) % bw == 0
            idx = (lambda i, j, start=start, bw=bw, nblk=width // bw:
                   (i, state.slot, 0, jnp.clip((j * tn - start) // bw, 0, nblk - 1)))
            out_specs.append(pl.BlockSpec((seqs, None, seq_len, bw), idx))
            out_shape.append(jax.ShapeDtypeStruct((m // seq_len, state.slots, seq_len, width), F32))
            if prev is not None:
                aliases[len(args)] = len(out_shape) - 1
                in_specs.append(pl.BlockSpec(memory_space=pl.ANY))
                args.append(prev)
    out = pl.pallas_call(
        functools.partial(_qkv_kernel, n_norm_cols=n_norm_cols, n_tiles=n // tn, rope=rope_tabs is not None,
                          state=None if state is None else state._replace(k_prev=None, v_prev=None),
                          n_aliased=len(aliases)),
        grid=(m // tm, n // tn),
        in_specs=in_specs,
        out_specs=out_specs,
        out_shape=out_shape,
        input_output_aliases=aliases,
        scratch_shapes=[pltpu.VMEM((tm, d), BF16), pltpu.VMEM((tm, 1), F32)],
        compiler_params=_params("parallel", "arbitrary"),
        name="adaln_qkv",
    )(*args)
    return out if state is not None else out[0]


def _attn_a_kernel(*refs, groups, window, tq, n_blocks, kv_per_step):
    if window:
        (sink_ref, q_ref, kp_ref, km_ref, kn_ref, kc_ref, vp_ref, vm_ref, vn_ref, vc_ref, o_ref) = refs
    else:
        sink_ref, q_ref, km_ref, vm_ref, o_ref = refs
    kvh0 = pl.program_id(2) * kv_per_step
    if window:
        i = pl.program_id(1)
        keys = [jnp.concatenate([kp_ref[...], km_ref[...], kn_ref[...], kc_ref[0]], axis=0)]
        vals = [jnp.concatenate([vp_ref[...], vm_ref[...], vn_ref[...], vc_ref[0]], axis=0)]
        n_lat = tq + 2 * A_WINDOW
        qi = lax.broadcasted_iota(jnp.int32, (tq, keys[0].shape[0]), 0)
        kj = lax.broadcasted_iota(jnp.int32, (tq, keys[0].shape[0]), 1)
        in_seq = ((i > 0) | (kj >= A_WINDOW)) & ((i < n_blocks - 1) | (kj < tq + A_WINDOW))
        mask = (kj >= n_lat) | ((jnp.abs(kj - A_WINDOW - qi) <= A_WINDOW) & in_seq)
    else:
        keys = [km_ref[:, kk * LANES:(kk + 1) * LANES].astype(BF16) for kk in range(kv_per_step)]
        vals = [vm_ref[:, kk * LANES:(kk + 1) * LANES].astype(BF16) for kk in range(kv_per_step)]
    n_heads = sink_ref.shape[1] - 1

    def heads(bounded):
        for kk in range(kv_per_step):
            for g in range(groups):
                sl = slice((kk * groups + g) * LANES, (kk * groups + g + 1) * LANES)
                q = q_ref[:, sl].astype(BF16)
                s = lax.dot_general(q, keys[kk], (((1,), (1,)), ((), ())), preferred_element_type=F32)
                if window:
                    s = jnp.where(mask, s, NEG_INF)
                sink = sink_ref[0, (kvh0 + kk) * groups + g] * LOG2E
                if bounded:
                    p = jnp.exp2(s)
                    denom = jnp.sum(p, axis=-1, keepdims=True) + jnp.exp2(jnp.full((1, 1), sink, F32))
                else:
                    m = jnp.maximum(jnp.max(s, axis=-1, keepdims=True), sink)
                    p = jnp.exp2(s - m)
                    denom = jnp.sum(p, axis=-1, keepdims=True) + jnp.exp2(sink - m)
                o = jnp.dot(p.astype(BF16), vals[kk], preferred_element_type=F32)
                o_ref[:, sl] = (o / denom).astype(o_ref.dtype)

    pl.when(sink_ref[0, n_heads] != 0.0)(functools.partial(heads, True))
    pl.when(sink_ref[0, n_heads] == 0.0)(functools.partial(heads, False))


def _attn_a(qkv, sink, batch, seq_len, heads, kv_heads, ctx_k=None, ctx_v=None, tq=256):
    m = qkv.shape[0]
    groups = heads // kv_heads
    window = ctx_k is not None
    tq = _tile(seq_len, tq)
    nb = seq_len // tq
    kps = 1 if window else kv_heads
    assert heads % kps == 0
    k_col, v_col = heads // kps, (heads + kv_heads) // kps
    r = tq // A_WINDOW
    rows128 = seq_len // A_WINDOW

    def main_rows(b, i):
        return b * nb + i

    def prev_rows(b, i):
        return b * rows128 + jnp.maximum(i * r - 1, 0)

    def next_rows(b, i):
        return b * rows128 + jnp.minimum((i + 1) * r, rows128 - 1)

    sink_spec = pl.BlockSpec(memory_space=pltpu.SMEM)
    q_spec = pl.BlockSpec((tq, kps * groups * LANES), lambda b, i, k: (main_rows(b, i), k))

    def kv_specs(col):
        main = pl.BlockSpec((tq, kps * LANES), lambda b, i, k: (main_rows(b, i), col + k))
        if not window:
            return [main]
        return [pl.BlockSpec((A_WINDOW, LANES), lambda b, i, k: (prev_rows(b, i), col + k)),
                main,
                pl.BlockSpec((A_WINDOW, LANES), lambda b, i, k: (next_rows(b, i), col + k)),
                pl.BlockSpec((1, ctx_k.shape[1], LANES), lambda b, i, k: (b, 0, k))]

    if window:
        args = [sink, qkv, qkv, qkv, qkv, ctx_k, qkv, qkv, qkv, ctx_v]
    else:
        args = [sink, qkv, qkv, qkv]
    return pl.pallas_call(
        functools.partial(_attn_a_kernel, groups=groups, window=window, tq=tq, n_blocks=nb, kv_per_step=kps),
        grid=(batch, nb, kv_heads // kps),
        in_specs=[sink_spec, q_spec] + kv_specs(k_col) + kv_specs(v_col),
        out_specs=pl.BlockSpec((tq, kps * groups * LANES), lambda b, i, k: (main_rows(b, i), k)),
        out_shape=jax.ShapeDtypeStruct((m, heads * LANES), BF16),
        compiler_params=_params("parallel", "arbitrary", "arbitrary"),
        name="attn_a_window" if window else "attn_a_ctx",
    )(*args)


SCORE_BOUND = 60.0


def _scores_bounded(q_gain, k_gain, head_dim, ctx_k=None):
    q_norm = math.sqrt(head_dim) * jnp.max(jnp.abs(q_gain))
    k_norm = math.sqrt(head_dim) * jnp.max(jnp.abs(k_gain))
    if ctx_k is not None:
        k_norm = jnp.maximum(k_norm, jnp.sqrt(jnp.max(jnp.sum(jnp.square(ctx_k.astype(F32)), axis=-1))))
    return (q_norm * k_norm <= SCORE_BOUND).astype(F32)


def _lane_group_sum(p):
    return functools.reduce(jnp.add, [p[:, t * LANES:(t + 1) * LANES] for t in range(p.shape[1] // LANES)])


def _attn_b_kernel(*refs, has_ctx, out_scale, ck):
    if has_ctx:
        scal_ref, q_ref, k_ref, v_ref, kc_ref, vc_ref, sg_ref, o_ref = refs
    else:
        scal_ref, q_ref, k_ref, v_ref, sg_ref, o_ref = refs
    chunks =[(k_ref, v_ref, s0) for s0 in range(0, k_ref.shape[0], ck)]
    if has_ctx:
        chunks += [(kc_ref.at[0], vc_ref.at[0], s0) for s0 in range(0, kc_ref.shape[1], ck)]

    hd = 2 * LANES
    n_heads = o_ref.shape[1] // hd

    def scores(col, kr, s0):
        q = q_ref[:, col:col + LANES].astype(BF16)
        k = kr[s0:s0 + ck, col:col + LANES].astype(BF16)
        return lax.dot_general(q, k, (((1,), (1,)), ((), ())), preferred_element_type=F32)

    def finish(base, state):
        outs = [acc / jnp.sum(l, axis=-1, keepdims=True) for l, acc in state]
        o = outs[0] - scal_ref[0, 0] * outs[1]
        o_ref[:, base:base + hd] = (_rms(o) * sg_ref[...] * out_scale).astype(o_ref.dtype)

    def bounded_head(base):
        state = [None, None]
        for kr, vr, s0 in chunks:
            v = vr[s0:s0 + ck, base:base + hd].astype(BF16)
            for c in range(2):
                p = jnp.exp2(scores(base + c * LANES, kr, s0))
                l, acc = _lane_group_sum(p), jnp.dot(p.astype(BF16), v, preferred_element_type=F32)
                state[c] = (l, acc) if state[c] is None else (state[c][0] + l, state[c][1] + acc)
        finish(base, state)

    def online_head(base):
        state = [None, None]
        for kr, vr, s0 in chunks:
            v = vr[s0:s0 + ck, base:base + hd].astype(BF16)
            for c in range(2):
                s = scores(base + c * LANES, kr, s0)
                row_max = jnp.max(s, axis=-1, keepdims=True)
                if state[c] is None:
                    p = jnp.exp2(s - row_max)
                    state[c] = (row_max, _lane_group_sum(p), jnp.dot(p.astype(BF16), v, preferred_element_type=F32))
                else:
                    m_old, l_old, acc_old = state[c]
                    m_new = jnp.maximum(m_old, row_max)
                    alpha = jnp.exp2(m_old - m_new)
                    p = jnp.exp2(s - m_new)
                    state[c] = (m_new, alpha * l_old + _lane_group_sum(p),
                                alpha * acc_old + jnp.dot(p.astype(BF16), v, preferred_element_type=F32))
        finish(base, [(l, acc) for _, l, acc in state])

    @pl.when(scal_ref[0, 1] != 0.0)
    def _():
        for hh in range(n_heads):
            bounded_head(hh * hd)

    @pl.when(scal_ref[0, 1] == 0.0)
    def _():
        for hh in range(n_heads):
            online_head(hh * hd)


def _attn_b(qkv, scal, subln_g, out_scale, batch, seq_len, heads, ctx_k=None, ctx_v=None, tq=1024, ck=512):
    m = qkv.shape[0]
    hd = 2 * LANES
    has_ctx = ctx_k is not None
    tq = _tile(seq_len, tq)
    ck = _tile(seq_len, ck, *([ctx_k.shape[1]] if has_ctx else []))
    nb = seq_len // tq
    hps = 1 if has_ctx else heads
    wd = hps * hd
    n_hblk = heads // hps
    in_specs = [pl.BlockSpec(memory_space=pltpu.SMEM),
                pl.BlockSpec((tq, wd), lambda b, h, i: (b * nb + i, h)),
                pl.BlockSpec((seq_len, wd), lambda b, h, i: (b, n_hblk + h)),
                pl.BlockSpec((seq_len, wd), lambda b, h, i: (b, 2 * n_hblk + h))]
    args = [scal, qkv, qkv, qkv]
    if has_ctx:
        in_specs += [pl.BlockSpec((1, ctx_k.shape[1], hd), lambda b, h, i: (b, 0, h))] * 2
        args += [ctx_k, ctx_v]
    in_specs.append(pl.BlockSpec((1, hd), lambda b, h, i: (0, 0)))
    args.append(subln_g)
    return pl.pallas_call(
        functools.partial(_attn_b_kernel, has_ctx=has_ctx, out_scale=out_scale, ck=ck),
        grid=(batch, n_hblk, nb),
        in_specs=in_specs,
        out_specs=pl.BlockSpec((tq, wd), lambda b, h, i: (b * nb + i, h)),
        out_shape=jax.ShapeDtypeStruct((m, heads * hd), BF16),
        compiler_params=_params("parallel", "arbitrary", "arbitrary"),
        name="attn_b_latent" if has_ctx else "attn_b_ctx",
    )(*args)


def _proj_residual_kernel(a_ref, w_ref, x_ref, gate_ref, o_ref):
    y = jnp.dot(a_ref[...], w_ref[...], preferred_element_type=F32)
    o_ref[...] = x_ref[...] + gate_ref[0] * y


def _proj_residual(a, w, layer, x, gate, seq_len, tm, tn):
    m, k = a.shape
    n = w.shape[2]
    per_batch = gate.shape[0] > 1
    tm = _tile(m, tm, seq_len) if per_batch else _tile(m, tm)
    tn = _tile(n, tn)
    mod_idx = _mod_index(per_batch, max(seq_len // tm, 1))
    gate_idx = lambda i, j: mod_idx(i, j)[:2] + (j,)
    return pl.pallas_call(
        _proj_residual_kernel,
        grid=(m // tm, n // tn),
        in_specs=[pl.BlockSpec((tm, k), lambda i, j: (i, 0)),
                  pl.BlockSpec((None, k, tn), lambda i, j: (layer, 0, j)),
                  pl.BlockSpec((tm, tn), lambda i, j: (i, j)),
                  pl.BlockSpec((1, 1, tn), gate_idx)],
        out_specs=pl.BlockSpec((tm, tn), lambda i, j: (i, j)),
        out_shape=jax.ShapeDtypeStruct((m, n), F32),
        compiler_params=_params("parallel", "arbitrary"),
        name="proj_residual",
    )(a, w, x, gate)


HALO = 16


def _ffn_kernel(xp_ref, x_ref, xn_ref, g_ref, sh_ref, sc_ref, gate_ref, wg_ref, wv_ref, cw_ref, cb_ref, wd_ref,
                o_ref, h_ref, inv_ref, *, seq_len, tm):
    i = pl.program_id(0)
    f = pl.program_id(1)

    @pl.when(f == 0)
    def _():
        inv_ref[...] = lax.rsqrt(jnp.mean(jnp.square(x_ref[...]), axis=-1, keepdims=True) + EPS)

    @pl.when(f == 0)
    def _():
        g, sh, sc = g_ref[...], sh_ref[0], sc_ref[0]
        h_ref[0:HALO] = _adaln(xp_ref[...], g, sh, sc).astype(BF16)
        h_ref[HALO:HALO + tm] = (x_ref[...] * inv_ref[...] * (g * (1.0 + sc)) + sh).astype(BF16)
        h_ref[HALO + tm:] = _adaln(xn_ref[...], g, sh, sc).astype(BF16)
        o_ref[...] = jnp.zeros_like(o_ref)

    gate_pre = jnp.dot(h_ref[...], wg_ref[...], preferred_element_type=F32)
    val = jnp.dot(h_ref[HALO:HALO + tm], wv_ref[...], preferred_element_type=F32)
    pos = (i * tm + lax.broadcasted_iota(jnp.int32, (tm, 1), 0)) & (seq_len - 1)
    up = jnp.where(pos != 0, gate_pre[HALO - 1:HALO - 1 + tm], 0.0)
    dn = jnp.where(pos != seq_len - 1, gate_pre[HALO + 1:HALO + 1 + tm], 0.0)
    gc = up * cw_ref[0:1] + gate_pre[HALO:HALO + tm] * cw_ref[1:2] + dn * cw_ref[2:3] + cb_ref[...]
    act = (gc / (1.0 + jnp.exp(-gc)) * val).astype(BF16)
    o_ref[...] += jnp.dot(act, wd_ref[...], preferred_element_type=F32)

    @pl.when(f == pl.num_programs(1) - 1)
    def _():
        o_ref[...] = x_ref[...] + gate_ref[0] * o_ref[...]


def _ffn(x, g, shift, scale, gate, w_up, conv_w, conv_b, w_down, layer, seq_len, tm, tf):
    m, d = x.shape
    ff = w_down.shape[1]
    per_batch = shift.shape[0] > 1
    tm = _tile(m, tm, seq_len) if per_batch else _tile(m, tm)
    tf = _tile(ff, tf)
    assert (seq_len % tm == 0 or tm % seq_len == 0) and tm % HALO == 0
    assert seq_len & (seq_len - 1) == 0, "token position uses a power-of-two mask"
    mod_idx = _mod_index(per_batch, max(seq_len // tm, 1))
    hb = tm // HALO
    n_halo_blocks = m // HALO
    nf = ff // tf
    return pl.pallas_call(
        functools.partial(_ffn_kernel, seq_len=seq_len, tm=tm),
        grid=(m // tm, nf),
        in_specs=[pl.BlockSpec((HALO, d), lambda i, f: (jnp.maximum(i * hb - 1, 0), 0)),
                  pl.BlockSpec((tm, d), lambda i, f: (i, 0)),
                  pl.BlockSpec((HALO, d), lambda i, f: (jnp.minimum((i + 1) * hb, n_halo_blocks - 1), 0)),
                  pl.BlockSpec((1, d), lambda i, f: (0, 0)),
                  pl.BlockSpec((1, 1, d), mod_idx),
                  pl.BlockSpec((1, 1, d), mod_idx),
                  pl.BlockSpec((1, 1, d), mod_idx),
                  pl.BlockSpec((None, d, tf), lambda i, f: (layer, 0, f)),
                  pl.BlockSpec((None, d, tf), lambda i, f: (layer, 0, nf + f)),
                  pl.BlockSpec((3, tf), lambda i, f: (0, f)),
                  pl.BlockSpec((1, tf), lambda i, f: (0, f)),
                  pl.BlockSpec((None, tf, d), lambda i, f: (layer, f, 0))],
        out_specs=pl.BlockSpec((tm, d), lambda i, f: (i, 0)),
        out_shape=jax.ShapeDtypeStruct((m, d), F32),
        scratch_shapes=[pltpu.VMEM((tm + 2 * HALO, d), BF16), pltpu.VMEM((tm, 1), F32)],
        compiler_params=_params("parallel", "arbitrary"),
        name="conv_glu",
    )(x, x, x, g, shift, scale, gate, w_up, w_up, conv_w, conv_b, w_down)


def _rope_tables(n, dim):
    rows = n // GRID_W
    row = jnp.repeat(jnp.arange(rows, dtype=F32), GRID_W)
    col = jnp.tile(jnp.arange(GRID_W, dtype=F32), rows)
    half = dim // 2
    inv = ROPE_BASE ** (-jnp.arange(0, half, 2, dtype=F32) / half)
    ar, ac = row[:, None] * inv, col[:, None] * inv
    cos = jnp.concatenate([jnp.cos(ar), jnp.cos(ar), jnp.cos(ac), jnp.cos(ac)], axis=-1)
    sin = jnp.concatenate([-jnp.sin(ar), jnp.sin(ar), -jnp.sin(ac), jnp.sin(ac)], axis=-1)
    reps = MXU_COLS // dim
    return jnp.tile(cos, (1, reps)), jnp.tile(sin, (1, reps))


def kernel(x_prompt, x_sample, cache_a_k, cache_a_v, cache_b_k, cache_b_v, c, c_ctx, ada_w, ada_b, norm1_g, norm2_g, a_w_qkv, a_q_norm, a_k_norm, a_sink, a_w_o, b_w_qkv, b_q_norm, b_k_norm, b_lambda_q1, b_lambda_k1, b_lambda_q2, b_lambda_k2, b_subln, b_w_o, ffn_w_up, ffn_conv_w, ffn_conv_b, ffn_w_down):
    batch, seq, d = x_prompt.shape
    dec_batch, dec_seq, _ = x_sample.shape
    depth = ada_w.shape[0]
    a_heads = a_sink.shape[1]
    a_kvh, a_hd = cache_a_k.shape[3], cache_a_k.shape[4]
    b_heads, b_qk = cache_b_k.shape[3], cache_b_k.shape[5]
    b_vd = cache_b_v.shape[4]
    past = cache_a_k.shape[2]
    assert a_hd == LANES and b_qk == LANES and b_vd == 2 * LANES

    cond = jnp.concatenate([c, c_ctx[None, :], jnp.zeros((16 - dec_batch - 1, d), F32)], axis=0)
    mod = _modulation(cond, ada_w, ada_b).reshape(depth, 16, 6, d)
    cos, sin = _rope_tables(dec_seq, LANES)

    xp = x_prompt.reshape(batch * seq, d)
    xs = x_sample.reshape(dec_batch * dec_seq, d)
    n_a, n_b = (depth + 1) // 2, depth // 2
    a_k_state = a_v_state = b_k_state = b_v_state = None
    a_wqkv, a_wo, b_wqkv, b_wo, w_up, w_down = [w.astype(BF16) for w in
                                               (a_w_qkv, a_w_o, b_w_qkv, b_w_o, ffn_w_up, ffn_w_down)]

    for l in range(depth):
        mod_s = [mod[l, :dec_batch, t][:, None, :] for t in range(6)]
        mod_p = [mod[l, dec_batch:dec_batch + 1, t][:, None, :] for t in range(6)]
        n1 = norm1_g[l][None, :]
        n2 = norm2_g[l][None, :]
        j = l // 2
        if l % 2 == 0:
            w_qkv, w_o = a_wqkv, a_wo
            qn, kn = a_heads * a_hd, a_kvh * a_hd
            q_gain = a_q_norm[j] * (a_hd ** -0.5 * LOG2E)
            head_gain = jnp.concatenate([jnp.tile(q_gain, a_heads),
                                         jnp.tile(a_k_norm[j], a_kvh), jnp.ones((kn,), F32)])[None, :]
            sink_ok = (jnp.max(jnp.abs(a_sink[j])) * LOG2E <= SCORE_BOUND).astype(F32)
            sink_p = jnp.concatenate([a_sink[j], sink_ok * _scores_bounded(q_gain, a_k_norm[j], a_hd)[None]])[None, :]
            sink_s = jnp.concatenate([a_sink[j], sink_ok * _scores_bounded(q_gain, a_k_norm[j], a_hd,
                                                                          cache_a_k[:, j])[None]])[None, :]
            qkv_p, a_k_state, a_v_state = _qkv(
                xp, n1, mod_p[0], mod_p[1], w_qkv, j, head_gain, qn + kn, seq, None, BF16, tm=512, tn=1024,
                state=_KvState(a_k_state, a_v_state, n_a, j, qn, kn, qn + kn, kn))
            op = _attn_a(qkv_p, sink_p, batch, seq, a_heads, a_kvh)
            qkv_s = _qkv(xs, n1, mod_s[0], mod_s[1], w_qkv, j, head_gain, qn + kn, dec_seq, (cos, sin), BF16,
                         tm=1024, tn=1024)
            ctx_k = cache_a_k[:, j].reshape(dec_batch, past, kn).astype(BF16)
            ctx_v = cache_a_v[:, j].reshape(dec_batch, past, kn).astype(BF16)
            os_ = _attn_a(qkv_s, sink_s, dec_batch, dec_seq, a_heads, a_kvh, ctx_k, ctx_v)
        else:
            w_qkv, w_o = b_wqkv, b_wo
            lambda_init = 0.8 - 0.6 * math.exp(-0.3 * l)
            lam = (jnp.exp(jnp.sum(b_lambda_q1[j] * b_lambda_k1[j])) - jnp.exp(jnp.sum(b_lambda_q2[j] * b_lambda_k2[j]))
                   + lambda_init)
            qn = b_heads * 2 * b_qk
            q_gain = b_q_norm[j] * (b_qk ** -0.5 * LOG2E)
            head_gain = jnp.concatenate([jnp.tile(q_gain, 2 * b_heads),
                                         jnp.tile(b_k_norm[j], 2 * b_heads), jnp.ones((qn,), F32)])[None, :]
            subln = b_subln[j][None, :]
            scal_p = jnp.stack([lam, _scores_bounded(q_gain, b_k_norm[j], b_qk)])[None, :]
            scal_s = jnp.stack([lam, _scores_bounded(q_gain, b_k_norm[j], b_qk, cache_b_k[:, j])])[None, :]
            qkv_p, b_k_state, b_v_state = _qkv(
                xp, n1, mod_p[0], mod_p[1], w_qkv, j, head_gain, 2 * qn, seq, None, BF16, tm=512, tn=1024,
                state=_KvState(b_k_state, b_v_state, n_b, j, qn, qn, 2 * qn, qn))
            op = _attn_b(qkv_p, scal_p, subln, 1.0 - lambda_init, batch, seq, b_heads)
            qkv_s = _qkv(xs, n1, mod_s[0], mod_s[1], w_qkv, j, head_gain, 2 * qn, dec_seq, (cos, sin), BF16,
                         tm=1024, tn=1024)
            ctx_k = cache_b_k[:, j].reshape(dec_batch, past, qn).astype(BF16)
            ctx_v = cache_b_v[:, j].reshape(dec_batch, past, qn).astype(BF16)
            os_ = _attn_b(qkv_s, scal_s, subln, 1.0 - lambda_init, dec_batch, dec_seq, b_heads, ctx_k, ctx_v)
        xp = _proj_residual(op, w_o, j, xp, mod_p[2], seq, tm=512, tn=2048)
        xs = _proj_residual(os_, w_o, j, xs, mod_s[2], dec_seq, tm=512, tn=2048)
        cb = ffn_conv_b[l][None, :]
        xp = _ffn(xp, n2, mod_p[3], mod_p[4], mod_p[5], w_up, ffn_conv_w[l], cb, w_down, l, seq, tm=1024, tf=512)
        xs = _ffn(xs, n2, mod_s[3], mod_s[4], mod_s[5], w_up, ffn_conv_w[l], cb, w_down, l, dec_seq, tm=1024, tf=512)

    return (xp.reshape(batch, seq, d), xs.reshape(dec_batch, dec_seq, d),
            a_k_state.reshape(batch, n_a, seq, a_kvh, a_hd), a_v_state.reshape(batch, n_a, seq, a_kvh, a_hd),
            b_k_state.reshape(batch, n_b, seq, b_heads, 2, b_qk), b_v_state.reshape(batch, n_b, seq, b_heads, b_vd))
```

```python
import functools
import math
from typing import NamedTuple, Optional

import jax
import jax.numpy as jnp
from jax import lax
from jax.experimental import pallas as pl
from jax.experimental.pallas import tpu as pltpu

GRID_W = 64
A_WINDOW = 128
ROPE_BASE = 10000.0
EPS = 1e-6
NEG_INF = -1e30
LANES = 128
MXU_COLS = 256
LOG2E = math.log2(math.e)
VMEM_LIMIT = 56 * 1024 * 1024

BF16 = jnp.bfloat16
F32 = jnp.float32


def _params(*sem):
    return pltpu.CompilerParams(dimension_semantics=sem, vmem_limit_bytes=VMEM_LIMIT)


def _tile(size, target, *also):
    t = min(target, size)
    while any(s % t for s in (size,) + also):
        t //= 2
    return t


def _rms(x):
    return x * lax.rsqrt(jnp.mean(x * x, axis=-1, keepdims=True) + EPS)


def _adaln(x, g, shift, scale):
    return _rms(x) * g * (1.0 + scale) + shift


def _mod_index(per_batch, tiles_per_seq):
    if per_batch:
        return lambda i, j: (i // tiles_per_seq, 0, 0)
    return lambda i, j: (0, 0, 0)


def _modulation_kernel(c_ref, w_ref, b_ref, o_ref):
    c = c_ref[...]
    s = (c / (1.0 + jnp.exp(-c))).astype(BF16)
    o_ref[0] = jnp.dot(s, w_ref[0].astype(BF16), preferred_element_type=F32) + b_ref[0]


def _modulation(cond, ada_w, ada_b, tn=1024):
    depth, d, n = ada_w.shape
    r = cond.shape[0]
    return pl.pallas_call(
        _modulation_kernel,
        grid=(depth, n // tn),
        in_specs=[pl.BlockSpec((r, d), lambda l, j: (0, 0)),
                  pl.BlockSpec((1, d, tn), lambda l, j: (l, 0, j)),
                  pl.BlockSpec((1, 1, tn), lambda l, j: (l, 0, j))],
        out_specs=pl.BlockSpec((1, r, tn), lambda l, j: (l, 0, j)),
        out_shape=jax.ShapeDtypeStruct((depth, r, n), F32),
        compiler_params=_params("arbitrary", "arbitrary"),
        name="modulation",
    )(cond, ada_w, ada_b.reshape(depth, 1, n))


def _head_matrices():
    i = jnp.arange(MXU_COLS)
    mean = jnp.where(i[:, None] // LANES == i[None, :] // LANES, 1.0 / LANES, 0.0)
    partner = jnp.where((i & 63) < 32, i + 32, i - 32)
    perm = (i[:, None] == partner[None, :])
    return mean.astype(BF16), perm.astype(BF16)


class _KvState(NamedTuple):
    k_prev: Optional[jax.Array]
    v_prev: Optional[jax.Array]
    slots: int
    slot: int
    k_start: int
    k_width: int
    v_start: int
    v_width: int


def _qkv_kernel(*refs, n_norm_cols, n_tiles, rope, state, n_aliased):
    n_in = 7 + (3 if rope else 0) + n_aliased
    x_ref, g_ref, sh_ref, sc_ref, w_ref, hg_ref, mean_ref = refs[:7]
    if rope:
        perm_ref, cos_ref, sin_ref = refs[7:10]
    if state is not None:
        o_ref, ks_ref, vs_ref, h_ref, inv_ref = refs[n_in:]
    else:
        o_ref, h_ref, inv_ref = refs[n_in:]
    j = pl.program_id(1)

    @pl.when(j == 0)
    def _():
        inv_ref[...] = lax.rsqrt(jnp.mean(jnp.square(x_ref[...]), axis=-1, keepdims=True) + EPS)

    @pl.when(j == 0)
    def _():
        gain = g_ref[...] * (1.0 + sc_ref[0])
        h_ref[...] = (x_ref[...] * inv_ref[...] * gain + sh_ref[0]).astype(BF16)

    tn = o_ref.shape[1]
    subs = tn // MXU_COLS
    y_all = jnp.dot(h_ref[...], w_ref[...], preferred_element_type=F32)

    def epilogue(tile):
        for c in range(subs):
            col = tile * tn + c * MXU_COLS
            sl = slice(c * MXU_COLS, (c + 1) * MXU_COLS)
            y = y_all[:, sl]
            if col < n_norm_cols:
                ms = jnp.dot((y * y).astype(BF16), mean_ref[...], preferred_element_type=F32)
                z = y * hg_ref[:, sl]
                if rope:
                    partner = jnp.dot(z.astype(BF16), perm_ref[...], preferred_element_type=F32)
                    z = z * cos_ref[...] + partner * sin_ref[...]
                y = z * lax.rsqrt(ms + EPS)
            o_ref[:, sl] = y.astype(o_ref.dtype)
            if state is not None:
                for ref, start, width in ((ks_ref, state.k_start, state.k_width), (vs_ref, state.v_start, state.v_width)):
                    if start <= col < start + width:
                        off = (col - start) % ref.shape[2]
                        ref[:, :, off:off + MXU_COLS] = y.reshape(ref.shape[0], ref.shape[1], MXU_COLS)

    for tile in range(n_tiles):
        pl.when(j == tile)(functools.partial(epilogue, tile))


def _qkv(x, g, shift, scale, w, layer, head_gain, n_norm_cols, seq_len, rope_tabs, out_dtype, tm, tn, state=None):
    m, d = x.shape
    n = w.shape[2]
    per_batch = shift.shape[0] > 1
    tn = _tile(n, tn)
    assert tn % MXU_COLS == 0 and n_norm_cols % MXU_COLS == 0
    tm = _tile(m, tm, seq_len) if (per_batch or rope_tabs is not None) else _tile(m, tm)
    tiles_per_seq = max(seq_len // tm, 1)
    mod_idx = _mod_index(per_batch, tiles_per_seq)
    mean_mat, perm_mat = _head_matrices()
    const_spec = pl.BlockSpec((MXU_COLS, MXU_COLS), lambda i, j: (0, 0))
    in_specs = [pl.BlockSpec((tm, d), lambda i, j: (i, 0)),
                pl.BlockSpec((1, d), lambda i, j: (0, 0)),
                pl.BlockSpec((1, 1, d), mod_idx),
                pl.BlockSpec((1, 1, d), mod_idx),
                pl.BlockSpec((None, d, tn), lambda i, j: (layer, 0, j)),
                pl.BlockSpec((1, tn), lambda i, j: (0, j)),
                const_spec]
    args = [x, g, shift, scale, w, head_gain, mean_mat]
    if rope_tabs is not None:
        in_specs += [const_spec] + [pl.BlockSpec((tm, MXU_COLS), lambda i, j: (i % tiles_per_seq, 0))] * 2
        args += [perm_mat] + list(rope_tabs)
    out_specs = [pl.BlockSpec((tm, tn), lambda i, j: (i, j))]
    out_shape = [jax.ShapeDtypeStruct((m, n), out_dtype)]
    aliases = {}
    if state is not None:
        assert tm % seq_len == 0 and not per_batch
        seqs = tm // seq_len
        for start, width, prev in ((state.k_start, state.k_width, state.k_prev),
                                   (state.v_start, state.v_width, state.v_prev)):
            bw = min(tn, width)
            assert width % bw == 0 and start % MXU_COLS == 0 and (start % tn) % bw == 0
            idx = (lambda i, j, start=start, bw=bw, nblk=width // bw:
                   (i, state.slot, 0, jnp.clip((j * tn - start) // bw, 0, nblk - 1)))
            out_specs.append(pl.BlockSpec((seqs, None, seq_len, bw), idx))
            out_shape.append(jax.ShapeDtypeStruct((m // seq_len, state.slots, seq_len, width), F32))
            if prev is not None:
                aliases[len(args)] = len(out_shape) - 1
                in_specs.append(pl.BlockSpec(memory_space=pl.ANY))
                args.append(prev)
    out = pl.pallas_call(
        functools.partial(_qkv_kernel, n_norm_cols=n_norm_cols, n_tiles=n // tn, rope=rope_tabs is not None,
                          state=None if state is None else state._replace(k_prev=None, v_prev=None),
                          n_aliased=len(aliases)),
        grid=(m // tm, n // tn),
        in_specs=in_specs,
        out_specs=out_specs,
        out_shape=out_shape,
        input_output_aliases=aliases,
        scratch_shapes=[pltpu.VMEM((tm, d), BF16), pltpu.VMEM((tm, 1), F32)],
        compiler_params=_params("parallel", "arbitrary"),
        name="adaln_qkv",
    )(*args)
    return out if state is not None else out[0]


def _window_bias(tq, n_ctx):
    n_lat = tq + 2 * A_WINDOW
    qi = jnp.arange(tq)[:, None]
    kj = jnp.arange(n_lat + n_ctx)[None, :]
    band = jnp.abs(kj - A_WINDOW - qi) <= A_WINDOW
    variants = []
    for last in (False, True):
        for first in (False, True):
            in_seq = ((kj >= A_WINDOW) if first else True) & ((kj < tq + A_WINDOW) if last else True)
            variants.append(jnp.where((kj >= n_lat) | (band & in_seq), 0.0, NEG_INF))
    return jnp.stack(variants).astype(F32)


def _attn_a_kernel(*refs, groups, window, kv_per_step):
    if window:
        (sink_ref, bias_ref, q_ref, kp_ref, km_ref, kn_ref, kc_ref, vp_ref, vm_ref, vn_ref, vc_ref, o_ref) = refs
    else:
        sink_ref, q_ref, km_ref, vm_ref, o_ref = refs
    kvh0 = pl.program_id(2) * kv_per_step
    if window:
        keys = [jnp.concatenate([kp_ref[...], km_ref[...], kn_ref[...], kc_ref[0]], axis=0)]
        vals = [jnp.concatenate([vp_ref[...], vm_ref[...], vn_ref[...], vc_ref[0]], axis=0)]
    else:
        keys = [km_ref[:, kk * LANES:(kk + 1) * LANES].astype(BF16) for kk in range(kv_per_step)]
        vals = [vm_ref[:, kk * LANES:(kk + 1) * LANES].astype(BF16) for kk in range(kv_per_step)]
    n_heads = sink_ref.shape[1] - 1

    def heads(bounded):
        for kk in range(kv_per_step):
            for g in range(groups):
                sl = slice((kk * groups + g) * LANES, (kk * groups + g + 1) * LANES)
                q = q_ref[:, sl].astype(BF16)
                s = lax.dot_general(q, keys[kk], (((1,), (1,)), ((), ())), preferred_element_type=F32)
                if window:
                    s = s + bias_ref[...]
                sink = sink_ref[0, (kvh0 + kk) * groups + g] * LOG2E
                if bounded:
                    p = jnp.exp2(s)
                    denom = jnp.sum(p, axis=-1, keepdims=True) + jnp.exp2(jnp.full((1, 1), sink, F32))
                else:
                    m = jnp.maximum(jnp.max(s, axis=-1, keepdims=True), sink)
                    p = jnp.exp2(s - m)
                    denom = jnp.sum(p, axis=-1, keepdims=True) + jnp.exp2(sink - m)
                o = jnp.dot(p.astype(BF16), vals[kk], preferred_element_type=F32)
                o_ref[:, sl] = (o / denom).astype(o_ref.dtype)

    pl.when(sink_ref[0, n_heads] != 0.0)(functools.partial(heads, True))
    pl.when(sink_ref[0, n_heads] == 0.0)(functools.partial(heads, False))


def _attn_a(qkv, sink, batch, seq_len, heads, kv_heads, ctx_k=None, ctx_v=None, tq=256):
    m = qkv.shape[0]
    groups = heads // kv_heads
    window = ctx_k is not None
    tq = _tile(seq_len, tq)
    nb = seq_len // tq
    kps = 1 if window else kv_heads
    assert heads % kps == 0
    k_col, v_col = heads // kps, (heads + kv_heads) // kps
    r = tq // A_WINDOW
    rows128 = seq_len // A_WINDOW

    def main_rows(b, i):
        return b * nb + i

    def prev_rows(b, i):
        return b * rows128 + jnp.maximum(i * r - 1, 0)

    def next_rows(b, i):
        return b * rows128 + jnp.minimum((i + 1) * r, rows128 - 1)

    sink_spec = pl.BlockSpec(memory_space=pltpu.SMEM)
    q_spec = pl.BlockSpec((tq, kps * groups * LANES), lambda b, i, k: (main_rows(b, i), k))

    def kv_specs(col):
        main = pl.BlockSpec((tq, kps * LANES), lambda b, i, k: (main_rows(b, i), col + k))
        if not window:
            return [main]
        return [pl.BlockSpec((A_WINDOW, LANES), lambda b, i, k: (prev_rows(b, i), col + k)),
                main,
                pl.BlockSpec((A_WINDOW, LANES), lambda b, i, k: (next_rows(b, i), col + k)),
                pl.BlockSpec((1, ctx_k.shape[1], LANES), lambda b, i, k: (b, 0, k))]

    lead_specs = [sink_spec]
    if window:
        bias = _window_bias(tq, ctx_k.shape[1])
        lead_specs.append(pl.BlockSpec((None,) + bias.shape[1:],
                                       lambda b, i, k: (jnp.where(i == 0, 1, 0) + jnp.where(i == nb - 1, 2, 0), 0, 0)))
        args = [sink, bias, qkv, qkv, qkv, qkv, ctx_k, qkv, qkv, qkv, ctx_v]
    else:
        args = [sink, qkv, qkv, qkv]
    return pl.pallas_call(
        functools.partial(_attn_a_kernel, groups=groups, window=window, kv_per_step=kps),
        grid=(batch, nb, kv_heads // kps),
        in_specs=lead_specs + [q_spec] + kv_specs(k_col) + kv_specs(v_col),
        out_specs=pl.BlockSpec((tq, kps * groups * LANES), lambda b, i, k: (main_rows(b, i), k)),
        out_shape=jax.ShapeDtypeStruct((m, heads * LANES), BF16),
        compiler_params=_params("parallel", "arbitrary", "arbitrary"),
        name="attn_a_window" if window else "attn_a_ctx",
    )(*args)


SCORE_BOUND = 40.0


def _scores_bounded(q_gain, k_gain, head_dim, ctx_k=None):
    q_norm = math.sqrt(head_dim) * jnp.max(jnp.abs(q_gain))
    k_norm = math.sqrt(head_dim) * jnp.max(jnp.abs(k_gain))
    if ctx_k is not None:
        k_norm = jnp.maximum(k_norm, jnp.sqrt(jnp.max(jnp.sum(jnp.square(ctx_k.astype(F32)), axis=-1))))
    return (q_norm * k_norm <= SCORE_BOUND).astype(F32)


def _lane_group_sum(p):
    return functools.reduce(jnp.add, [p[:, t * LANES:(t + 1) * LANES] for t in range(p.shape[1] // LANES)])


def _attn_b_kernel(*refs, has_ctx, out_scale, ck):
    if has_ctx:
        scal_ref, q_ref, k_ref, v_ref, kc_ref, vc_ref, sg_ref, o_ref = refs
    else:
        scal_ref, q_ref, k_ref, v_ref, sg_ref, o_ref = refs
    chunks =[(k_ref, v_ref, s0) for s0 in range(0, k_ref.shape[0], ck)]
    if has_ctx:
        chunks += [(kc_ref.at[0], vc_ref.at[0], s0) for s0 in range(0, kc_ref.shape[1], ck)]

    hd = 2 * LANES
    n_heads = o_ref.shape[1] // hd

    def scores(col, kr, s0):
        q = q_ref[:, col:col + LANES].astype(BF16)
        k = kr[s0:s0 + ck, col:col + LANES].astype(BF16)
        return lax.dot_general(q, k, (((1,), (1,)), ((), ())), preferred_element_type=F32)

    def finish(base, state):
        outs = [acc / jnp.sum(l, axis=-1, keepdims=True) for l, acc in state]
        o = outs[0] - scal_ref[0, 0] * outs[1]
        o_ref[:, base:base + hd] = (_rms(o) * sg_ref[...] * out_scale).astype(o_ref.dtype)

    def bounded_head(base):
        state = [None, None]
        for kr, vr, s0 in chunks:
            v = vr[s0:s0 + ck, base:base + hd].astype(BF16)
            for c in range(2):
                p = jnp.exp2(scores(base + c * LANES, kr, s0))
                l, acc = _lane_group_sum(p), jnp.dot(p.astype(BF16), v, preferred_element_type=F32)
                state[c] = (l, acc) if state[c] is None else (state[c][0] + l, state[c][1] + acc)
        finish(base, state)

    def online_head(base):
        state = [None, None]
        for kr, vr, s0 in chunks:
            v = vr[s0:s0 + ck, base:base + hd].astype(BF16)
            for c in range(2):
                s = scores(base + c * LANES, kr, s0)
                row_max = jnp.max(s, axis=-1, keepdims=True)
                if state[c] is None:
                    p = jnp.exp2(s - row_max)
                    state[c] = (row_max, _lane_group_sum(p), jnp.dot(p.astype(BF16), v, preferred_element_type=F32))
                else:
                    m_old, l_old, acc_old = state[c]
                    m_new = jnp.maximum(m_old, row_max)
                    alpha = jnp.exp2(m_old - m_new)
                    p = jnp.exp2(s - m_new)
                    state[c] = (m_new, alpha * l_old + _lane_group_sum(p),
                                alpha * acc_old + jnp.dot(p.astype(BF16), v, preferred_element_type=F32))
        finish(base, [(l, acc) for _, l, acc in state])

    @pl.when(scal_ref[0, 1] != 0.0)
    def _():
        for hh in range(n_heads):
            bounded_head(hh * hd)

    @pl.when(scal_ref[0, 1] == 0.0)
    def _():
        for hh in range(n_heads):
            online_head(hh * hd)


def _attn_b(qkv, scal, subln_g, out_scale, batch, seq_len, heads, ctx_k=None, ctx_v=None, tq=1024, ck=512):
    m = qkv.shape[0]
    hd = 2 * LANES
    has_ctx = ctx_k is not None
    tq = _tile(seq_len, tq)
    ck = _tile(seq_len, ck, *([ctx_k.shape[1]] if has_ctx else []))
    nb = seq_len // tq
    hps = 1 if has_ctx else heads
    wd = hps * hd
    n_hblk = heads // hps
    in_specs = [pl.BlockSpec(memory_space=pltpu.SMEM),
                pl.BlockSpec((tq, wd), lambda b, h, i: (b * nb + i, h)),
                pl.BlockSpec((seq_len, wd), lambda b, h, i: (b, n_hblk + h)),
                pl.BlockSpec((seq_len, wd), lambda b, h, i: (b, 2 * n_hblk + h))]
    args = [scal, qkv, qkv, qkv]
    if has_ctx:
        in_specs += [pl.BlockSpec((1, ctx_k.shape[1], hd), lambda b, h, i: (b, 0, h))] * 2
        args += [ctx_k, ctx_v]
    in_specs.append(pl.BlockSpec((1, hd), lambda b, h, i: (0, 0)))
    args.append(subln_g)
    return pl.pallas_call(
        functools.partial(_attn_b_kernel, has_ctx=has_ctx, out_scale=out_scale, ck=ck),
        grid=(batch, n_hblk, nb),
        in_specs=in_specs,
        out_specs=pl.BlockSpec((tq, wd), lambda b, h, i: (b * nb + i, h)),
        out_shape=jax.ShapeDtypeStruct((m, heads * hd), BF16),
        compiler_params=_params("parallel", "arbitrary", "arbitrary"),
        name="attn_b_latent" if has_ctx else "attn_b_ctx",
    )(*args)


def _proj_residual_kernel(a_ref, w_ref, x_ref, gate_ref, o_ref):
    y = jnp.dot(a_ref[...], w_ref[...], preferred_element_type=F32)
    o_ref[...] = x_ref[...] + gate_ref[0] * y


def _proj_residual(a, w, layer, x, gate, seq_len, tm, tn):
    m, k = a.shape
    n = w.shape[2]
    per_batch = gate.shape[0] > 1
    tm = _tile(m, tm, seq_len) if per_batch else _tile(m, tm)
    tn = _tile(n, tn)
    mod_idx = _mod_index(per_batch, max(seq_len // tm, 1))
    gate_idx = lambda i, j: mod_idx(i, j)[:2] + (j,)
    return pl.pallas_call(
        _proj_residual_kernel,
        grid=(m // tm, n // tn),
        in_specs=[pl.BlockSpec((tm, k), lambda i, j: (i, 0)),
                  pl.BlockSpec((None, k, tn), lambda i, j: (layer, 0, j)),
                  pl.BlockSpec((tm, tn), lambda i, j: (i, j)),
                  pl.BlockSpec((1, 1, tn), gate_idx)],
        out_specs=pl.BlockSpec((tm, tn), lambda i, j: (i, j)),
        out_shape=jax.ShapeDtypeStruct((m, n), F32),
        compiler_params=_params("parallel", "arbitrary"),
        name="proj_residual",
    )(a, w, x, gate)


HALO = 16


def _ffn_kernel(xp_ref, x_ref, xn_ref, g_ref, sh_ref, sc_ref, gate_ref, wg_ref, wv_ref, cw_ref, cb_ref, wd_ref,
                o_ref, h_ref, inv_ref, *, seq_len, tm):
    i = pl.program_id(0)
    f = pl.program_id(1)

    @pl.when(f == 0)
    def _():
        inv_ref[...] = lax.rsqrt(jnp.mean(jnp.square(x_ref[...]), axis=-1, keepdims=True) + EPS)

    @pl.when(f == 0)
    def _():
        g, sh, sc = g_ref[...], sh_ref[0], sc_ref[0]
        h_ref[0:HALO] = _adaln(xp_ref[...], g, sh, sc).astype(BF16)
        h_ref[HALO:HALO + tm] = (x_ref[...] * inv_ref[...] * (g * (1.0 + sc)) + sh).astype(BF16)
        h_ref[HALO + tm:] = _adaln(xn_ref[...], g, sh, sc).astype(BF16)
        o_ref[...] = jnp.zeros_like(o_ref)

    gate_pre = jnp.dot(h_ref[...], wg_ref[...], preferred_element_type=F32)
    val = jnp.dot(h_ref[HALO:HALO + tm], wv_ref[...], preferred_element_type=F32)
    pos = (i * tm + lax.broadcasted_iota(jnp.int32, (tm, 1), 0)) & (seq_len - 1)
    up = jnp.where(pos != 0, gate_pre[HALO - 1:HALO - 1 + tm], 0.0)
    dn = jnp.where(pos != seq_len - 1, gate_pre[HALO + 1:HALO + 1 + tm], 0.0)
    gc = up * cw_ref[0:1] + gate_pre[HALO:HALO + tm] * cw_ref[1:2] + dn * cw_ref[2:3] + cb_ref[...]
    act = (gc / (1.0 + jnp.exp(-gc)) * val).astype(BF16)
    o_ref[...] += jnp.dot(act, wd_ref[...], preferred_element_type=F32)

    @pl.when(f == pl.num_programs(1) - 1)
    def _():
        o_ref[...] = x_ref[...] + gate_ref[0] * o_ref[...]


def _ffn(x, g, shift, scale, gate, w_up, conv_w, conv_b, w_down, layer, seq_len, tm, tf):
    m, d = x.shape
    ff = w_down.shape[1]
    per_batch = shift.shape[0] > 1
    tm = _tile(m, tm, seq_len) if per_batch else _tile(m, tm)
    tf = _tile(ff, tf)
    assert (seq_len % tm == 0 or tm % seq_len == 0) and tm % HALO == 0
    assert seq_len & (seq_len - 1) == 0, "token position uses a power-of-two mask"
    mod_idx = _mod_index(per_batch, max(seq_len // tm, 1))
    hb = tm // HALO
    n_halo_blocks = m // HALO
    nf = ff // tf
    return pl.pallas_call(
        functools.partial(_ffn_kernel, seq_len=seq_len, tm=tm),
        grid=(m // tm, nf),
        in_specs=[pl.BlockSpec((HALO, d), lambda i, f: (jnp.maximum(i * hb - 1, 0), 0)),
                  pl.BlockSpec((tm, d), lambda i, f: (i, 0)),
                  pl.BlockSpec((HALO, d), lambda i, f: (jnp.minimum((i + 1) * hb, n_halo_blocks - 1), 0)),
                  pl.BlockSpec((1, d), lambda i, f: (0, 0)),
                  pl.BlockSpec((1, 1, d), mod_idx),
                  pl.BlockSpec((1, 1, d), mod_idx),
                  pl.BlockSpec((1, 1, d), mod_idx),
                  pl.BlockSpec((None, d, tf), lambda i, f: (layer, 0, f)),
                  pl.BlockSpec((None, d, tf), lambda i, f: (layer, 0, nf + f)),
                  pl.BlockSpec((3, tf), lambda i, f: (0, f)),
                  pl.BlockSpec((1, tf), lambda i, f: (0, f)),
                  pl.BlockSpec((None, tf, d), lambda i, f: (layer, f, 0))],
        out_specs=pl.BlockSpec((tm, d), lambda i, f: (i, 0)),
        out_shape=jax.ShapeDtypeStruct((m, d), F32),
        scratch_shapes=[pltpu.VMEM((tm + 2 * HALO, d), BF16), pltpu.VMEM((tm, 1), F32)],
        compiler_params=_params("parallel", "arbitrary"),
        name="conv_glu",
    )(x, x, x, g, shift, scale, gate, w_up, w_up, conv_w, conv_b, w_down)


def _rope_tables(n, dim):
    rows = n // GRID_W
    row = jnp.repeat(jnp.arange(rows, dtype=F32), GRID_W)
    col = jnp.tile(jnp.arange(GRID_W, dtype=F32), rows)
    half = dim // 2
    inv = ROPE_BASE ** (-jnp.arange(0, half, 2, dtype=F32) / half)
    ar, ac = row[:, None] * inv, col[:, None] * inv
    cos = jnp.concatenate([jnp.cos(ar), jnp.cos(ar), jnp.cos(ac), jnp.cos(ac)], axis=-1)
    sin = jnp.concatenate([-jnp.sin(ar), jnp.sin(ar), -jnp.sin(ac), jnp.sin(ac)], axis=-1)
    reps = MXU_COLS // dim
    return jnp.tile(cos, (1, reps)), jnp.tile(sin, (1, reps))


def kernel(x_prompt, x_sample, cache_a_k, cache_a_v, cache_b_k, cache_b_v, c, c_ctx, ada_w, ada_b, norm1_g, norm2_g, a_w_qkv, a_q_norm, a_k_norm, a_sink, a_w_o, b_w_qkv, b_q_norm, b_k_norm, b_lambda_q1, b_lambda_k1, b_lambda_q2, b_lambda_k2, b_subln, b_w_o, ffn_w_up, ffn_conv_w, ffn_conv_b, ffn_w_down):
    batch, seq, d = x_prompt.shape
    dec_batch, dec_seq, _ = x_sample.shape
    depth = ada_w.shape[0]
    a_heads = a_sink.shape[1]
    a_kvh, a_hd = cache_a_k.shape[3], cache_a_k.shape[4]
    b_heads, b_qk = cache_b_k.shape[3], cache_b_k.shape[5]
    b_vd = cache_b_v.shape[4]
    past = cache_a_k.shape[2]
    assert a_hd == LANES and b_qk == LANES and b_vd == 2 * LANES

    cond = jnp.concatenate([c, c_ctx[None, :], jnp.zeros((16 - dec_batch - 1, d), F32)], axis=0)
    mod = _modulation(cond, ada_w, ada_b).reshape(depth, 16, 6, d)
    cos, sin = _rope_tables(dec_seq, LANES)

    xp = x_prompt.reshape(batch * seq, d)
    xs = x_sample.reshape(dec_batch * dec_seq, d)
    n_a, n_b = (depth + 1) // 2, depth // 2
    a_k_state = a_v_state = b_k_state = b_v_state = None
    a_wqkv, a_wo, b_wqkv, b_wo, w_up, w_down = [w.astype(BF16) for w in
                                               (a_w_qkv, a_w_o, b_w_qkv, b_w_o, ffn_w_up, ffn_w_down)]

    for l in range(depth):
        mod_s = [mod[l, :dec_batch, t][:, None, :] for t in range(6)]
        mod_p = [mod[l, dec_batch:dec_batch + 1, t][:, None, :] for t in range(6)]
        n1 = norm1_g[l][None, :]
        n2 = norm2_g[l][None, :]
        j = l // 2
        if l % 2 == 0:
            w_qkv, w_o = a_wqkv, a_wo
            qn, kn = a_heads * a_hd, a_kvh * a_hd
            q_gain = a_q_norm[j] * (a_hd ** -0.5 * LOG2E)
            head_gain = jnp.concatenate([jnp.tile(q_gain, a_heads),
                                         jnp.tile(a_k_norm[j], a_kvh), jnp.ones((kn,), F32)])[None, :]
            sink_ok = (jnp.max(jnp.abs(a_sink[j])) * LOG2E <= SCORE_BOUND).astype(F32)
            sink_p = jnp.concatenate([a_sink[j], sink_ok * _scores_bounded(q_gain, a_k_norm[j], a_hd)[None]])[None, :]
            sink_s = jnp.concatenate([a_sink[j], sink_ok * _scores_bounded(q_gain, a_k_norm[j], a_hd,
                                                                          cache_a_k[:, j])[None]])[None, :]
            qkv_p, a_k_state, a_v_state = _qkv(
                xp, n1, mod_p[0], mod_p[1], w_qkv, j, head_gain, qn + kn, seq, None, BF16, tm=1024, tn=1024,
                state=_KvState(a_k_state, a_v_state, n_a, j, qn, kn, qn + kn, kn))
            op = _attn_a(qkv_p, sink_p, batch, seq, a_heads, a_kvh)
            qkv_s = _qkv(xs, n1, mod_s[0], mod_s[1], w_qkv, j, head_gain, qn + kn, dec_seq, (cos, sin), BF16,
                         tm=1024, tn=1536)
            ctx_k = cache_a_k[:, j].reshape(dec_batch, past, kn).astype(BF16)
            ctx_v = cache_a_v[:, j].reshape(dec_batch, past, kn).astype(BF16)
            os_ = _attn_a(qkv_s, sink_s, dec_batch, dec_seq, a_heads, a_kvh, ctx_k, ctx_v)
        else:
            w_qkv, w_o = b_wqkv, b_wo
            lambda_init = 0.8 - 0.6 * math.exp(-0.3 * l)
            lam = (jnp.exp(jnp.sum(b_lambda_q1[j] * b_lambda_k1[j])) - jnp.exp(jnp.sum(b_lambda_q2[j] * b_lambda_k2[j]))
                   + lambda_init)
            qn = b_heads * 2 * b_qk
            q_gain = b_q_norm[j] * (b_qk ** -0.5 * LOG2E)
            head_gain = jnp.concatenate([jnp.tile(q_gain, 2 * b_heads),
                                         jnp.tile(b_k_norm[j], 2 * b_heads), jnp.ones((qn,), F32)])[None, :]
            subln = b_subln[j][None, :]
            scal_p = jnp.stack([lam, _scores_bounded(q_gain, b_k_norm[j], b_qk)])[None, :]
            scal_s = jnp.stack([lam, _scores_bounded(q_gain, b_k_norm[j], b_qk, cache_b_k[:, j])])[None, :]
            qkv_p, b_k_state, b_v_state = _qkv(
                xp, n1, mod_p[0], mod_p[1], w_qkv, j, head_gain, 2 * qn, seq, None, BF16, tm=1024, tn=1024,
                state=_KvState(b_k_state, b_v_state, n_b, j, qn, qn, 2 * qn, qn))
            op = _attn_b(qkv_p, scal_p, subln, 1.0 - lambda_init, batch, seq, b_heads)
            qkv_s = _qkv(xs, n1, mod_s[0], mod_s[1], w_qkv, j, head_gain, 2 * qn, dec_seq, (cos, sin), BF16,
                         tm=1024, tn=1536)
            ctx_k = cache_b_k[:, j].reshape(dec_batch, past, qn).astype(BF16)
            ctx_v = cache_b_v[:, j].reshape(dec_batch, past, qn).astype(BF16)
            os_ = _attn_b(qkv_s, scal_s, subln, 1.0 - lambda_init, dec_batch, dec_seq, b_heads, ctx_k, ctx_v)
        xp = _proj_residual(op, w_o, j, xp, mod_p[2], seq, tm=512, tn=2048)
        xs = _proj_residual(os_, w_o, j, xs, mod_s[2], dec_seq, tm=512, tn=2048)
        cb = ffn_conv_b[l][None, :]
        xp = _ffn(xp, n2, mod_p[3], mod_p[4], mod_p[5], w_up, ffn_conv_w[l], cb, w_down, l, seq, tm=1024, tf=512)
        xs = _ffn(xs, n2, mod_s[3], mod_s[4], mod_s[5], w_up, ffn_conv_w[l], cb, w_down, l, dec_seq, tm=1024, tf=512)

    return (xp.reshape(batch, seq, d), xs.reshape(dec_batch, dec_seq, d),
            a_k_state.reshape(batch, n_a, seq, a_kvh, a_hd), a_v_state.reshape(batch, n_a, seq, a_kvh, a_hd),
            b_k_state.reshape(batch, n_b, seq, b_heads, 2, b_qk), b_v_state.reshape(batch, n_b, seq, b_heads, b_vd))
```

```python
import functools
import math
from typing import NamedTuple, Optional

import jax
import jax.numpy as jnp
from jax import lax
from jax.experimental import pallas as pl
from jax.experimental.pallas import tpu as pltpu

GRID_W = 64
A_WINDOW = 128
ROPE_BASE = 10000.0
EPS = 1e-6
NEG_INF = -1e30
LANES = 128
MXU_COLS = 256
BF16_ROWS = 16
LOG2E = math.log2(math.e)
VMEM_LIMIT = 56 * 1024 * 1024

BF16 = jnp.bfloat16
F32 = jnp.float32


class _Tiles(NamedTuple):
    modulation_cols: int = 1024
    qkv_rows: int = 1024
    qkv_cols_latent: int = 1536
    qkv_cols_context: int = 1024
    proj_rows: int = 512
    proj_cols: int = 2048
    ffn_rows: int = 1024
    ffn_cols: int = 512
    attn_a_queries: int = 256
    attn_b_queries: int = 1024
    attn_b_keys: int = 512


TILES = _Tiles()


def _params(*sem):
    return pltpu.CompilerParams(dimension_semantics=sem, vmem_limit_bytes=VMEM_LIMIT)


def _tile(size, target, *also):
    t = min(target, size)
    while any(s % t for s in (size,) + also):
        t //= 2
    return t


def _rms(x):
    return x * lax.rsqrt(jnp.mean(x * x, axis=-1, keepdims=True) + EPS)


def _adaln(x, g, shift, scale):
    return _rms(x) * g * (1.0 + scale) + shift


def _mod_index(per_batch, tiles_per_seq):
    if per_batch:
        return lambda i, j: (i // tiles_per_seq, 0, 0)
    return lambda i, j: (0, 0, 0)


def _modulation_kernel(c_ref, w_ref, b_ref, o_ref):
    c = c_ref[...]
    s = (c / (1.0 + jnp.exp(-c))).astype(BF16)
    o_ref[0] = jnp.dot(s, w_ref[0].astype(BF16), preferred_element_type=F32) + b_ref[0]


def _modulation(cond, ada_w, ada_b, tn=TILES.modulation_cols):
    depth, d, n = ada_w.shape
    r = cond.shape[0]
    return pl.pallas_call(
        _modulation_kernel,
        grid=(depth, n // tn),
        in_specs=[pl.BlockSpec((r, d), lambda l, j: (0, 0)),
                  pl.BlockSpec((1, d, tn), lambda l, j: (l, 0, j)),
                  pl.BlockSpec((1, 1, tn), lambda l, j: (l, 0, j))],
        out_specs=pl.BlockSpec((1, r, tn), lambda l, j: (l, 0, j)),
        out_shape=jax.ShapeDtypeStruct((depth, r, n), F32),
        compiler_params=_params("arbitrary", "arbitrary"),
        name="modulation",
    )(cond, ada_w, ada_b.reshape(depth, 1, n))


def _head_matrices():
    i = jnp.arange(MXU_COLS)
    mean = jnp.where(i[:, None] // LANES == i[None, :] // LANES, 1.0 / LANES, 0.0)
    quarter = LANES // 4
    partner = jnp.where(i % (2 * quarter) < quarter, i + quarter, i - quarter)
    perm = (i[:, None] == partner[None, :])
    return mean.astype(BF16), perm.astype(BF16)


class _KvState(NamedTuple):
    k_prev: Optional[jax.Array]
    v_prev: Optional[jax.Array]
    slots: int
    slot: int
    k_start: int
    k_width: int
    v_start: int
    v_width: int


def _qkv_kernel(*refs, n_norm_cols, n_tiles, rope, state, n_aliased):
    n_in = 7 + (3 if rope else 0) + n_aliased
    x_ref, g_ref, sh_ref, sc_ref, w_ref, hg_ref, mean_ref = refs[:7]
    if rope:
        perm_ref, cos_ref, sin_ref = refs[7:10]
    if state is not None:
        o_ref, ks_ref, vs_ref, h_ref, inv_ref = refs[n_in:]
    else:
        o_ref, h_ref, inv_ref = refs[n_in:]
    j = pl.program_id(1)

    @pl.when(j == 0)
    def _():
        inv_ref[...] = lax.rsqrt(jnp.mean(jnp.square(x_ref[...]), axis=-1, keepdims=True) + EPS)

    @pl.when(j == 0)
    def _():
        gain = g_ref[...] * (1.0 + sc_ref[0])
        h_ref[...] = (x_ref[...] * inv_ref[...] * gain + sh_ref[0]).astype(BF16)

    tn = o_ref.shape[1]
    subs = tn // MXU_COLS
    y_all = jnp.dot(h_ref[...], w_ref[...], preferred_element_type=F32)

    def epilogue(tile):
        for c in range(subs):
            col = tile * tn + c * MXU_COLS
            sl = slice(c * MXU_COLS, (c + 1) * MXU_COLS)
            y = y_all[:, sl]
            if col < n_norm_cols:
                ms = jnp.dot((y * y).astype(BF16), mean_ref[...], preferred_element_type=F32)
                z = y * hg_ref[:, sl]
                if rope:
                    partner = jnp.dot(z.astype(BF16), perm_ref[...], preferred_element_type=F32)
                    z = z * cos_ref[...] + partner * sin_ref[...]
                y = z * lax.rsqrt(ms + EPS)
            o_ref[:, sl] = y.astype(o_ref.dtype)
            if state is not None:
                for ref, start, width in ((ks_ref, state.k_start, state.k_width), (vs_ref, state.v_start, state.v_width)):
                    if start <= col < start + width:
                        off = (col - start) % ref.shape[2]
                        ref[:, :, off:off + MXU_COLS] = y.reshape(ref.shape[0], ref.shape[1], MXU_COLS)

    for tile in range(n_tiles):
        pl.when(j == tile)(functools.partial(epilogue, tile))


def _qkv(x, g, shift, scale, w, layer, head_gain, n_norm_cols, seq_len, rope_tabs, out_dtype, tm, tn, state=None):
    m, d = x.shape
    n = w.shape[2]
    per_batch = shift.shape[0] > 1
    tn = _tile(n, tn)
    assert tn % MXU_COLS == 0 and n_norm_cols % MXU_COLS == 0
    tm = _tile(m, tm, seq_len) if (per_batch or rope_tabs is not None) else _tile(m, tm)
    tiles_per_seq = max(seq_len // tm, 1)
    mod_idx = _mod_index(per_batch, tiles_per_seq)
    mean_mat, perm_mat = _head_matrices()
    const_spec = pl.BlockSpec((MXU_COLS, MXU_COLS), lambda i, j: (0, 0))
    in_specs = [pl.BlockSpec((tm, d), lambda i, j: (i, 0)),
                pl.BlockSpec((1, d), lambda i, j: (0, 0)),
                pl.BlockSpec((1, 1, d), mod_idx),
                pl.BlockSpec((1, 1, d), mod_idx),
                pl.BlockSpec((None, d, tn), lambda i, j: (layer, 0, j)),
                pl.BlockSpec((1, tn), lambda i, j: (0, j)),
                const_spec]
    args = [x, g, shift, scale, w, head_gain, mean_mat]
    if rope_tabs is not None:
        in_specs += [const_spec] + [pl.BlockSpec((tm, MXU_COLS), lambda i, j: (i % tiles_per_seq, 0))] * 2
        args += [perm_mat] + list(rope_tabs)
    out_specs = [pl.BlockSpec((tm, tn), lambda i, j: (i, j))]
    out_shape = [jax.ShapeDtypeStruct((m, n), out_dtype)]
    aliases = {}
    if state is not None:
        assert tm % seq_len == 0 and not per_batch
        seqs = tm // seq_len
        for start, width, prev in ((state.k_start, state.k_width, state.k_prev),
                                   (state.v_start, state.v_width, state.v_prev)):
            bw = min(tn, width)
            assert width % bw == 0 and start % MXU_COLS == 0 and (start % tn) % bw == 0
            idx = (lambda i, j, start=start, bw=bw, nblk=width // bw:
                   (i, state.slot, 0, jnp.clip((j * tn - start) // bw, 0, nblk - 1)))
            out_specs.append(pl.BlockSpec((seqs, None, seq_len, bw), idx))
            out_shape.append(jax.ShapeDtypeStruct((m // seq_len, state.slots, seq_len, width), F32))
            if prev is not None:
                aliases[len(args)] = len(out_shape) - 1
                in_specs.append(pl.BlockSpec(memory_space=pl.ANY))
                args.append(prev)
    out = pl.pallas_call(
        functools.partial(_qkv_kernel, n_norm_cols=n_norm_cols, n_tiles=n // tn, rope=rope_tabs is not None,
                          state=None if state is None else state._replace(k_prev=None, v_prev=None),
                          n_aliased=len(aliases)),
        grid=(m // tm, n // tn),
        in_specs=in_specs,
        out_specs=out_specs,
        out_shape=out_shape,
        input_output_aliases=aliases,
        scratch_shapes=[pltpu.VMEM((tm, d), BF16), pltpu.VMEM((tm, 1), F32)],
        compiler_params=_params("parallel", "arbitrary"),
        name="adaln_qkv",
    )(*args)
    return out if state is not None else out[0]


def _window_bias(tq, n_ctx):
    n_lat = tq + 2 * A_WINDOW
    qi = jnp.arange(tq)[:, None]
    kj = jnp.arange(n_lat + n_ctx)[None, :]
    band = jnp.abs(kj - A_WINDOW - qi) <= A_WINDOW
    variants = []
    for last in (False, True):
        for first in (False, True):
            in_seq = ((kj >= A_WINDOW) if first else True) & ((kj < tq + A_WINDOW) if last else True)
            variants.append(jnp.where((kj >= n_lat) | (band & in_seq), 0.0, NEG_INF))
    return jnp.stack(variants).astype(F32)


def _attn_a_kernel(*refs, groups, window, kv_per_step):
    if window:
        (sink_ref, bias_ref, q_ref, kp_ref, km_ref, kn_ref, kc_ref, vp_ref, vm_ref, vn_ref, vc_ref, o_ref) = refs
    else:
        sink_ref, q_ref, km_ref, vm_ref, o_ref = refs
    kvh0 = pl.program_id(2) * kv_per_step
    def head_rows(refs_, kk):
        cols = slice(kk * LANES, (kk + 1) * LANES)
        pieces = [r[0, :, cols] if len(r.shape) == 3 else r[:, cols] for r in refs_]
        return pieces[0].astype(BF16) if len(pieces) == 1 else jnp.concatenate(pieces, axis=0)

    k_refs = (kp_ref, km_ref, kn_ref, kc_ref) if window else (km_ref,)
    v_refs = (vp_ref, vm_ref, vn_ref, vc_ref) if window else (vm_ref,)
    keys = [head_rows(k_refs, kk) for kk in range(kv_per_step)]
    vals = [head_rows(v_refs, kk) for kk in range(kv_per_step)]
    n_heads = sink_ref.shape[1] - 1

    def heads(bounded):
        for kk in range(kv_per_step):
            for g in range(groups):
                sl = slice((kk * groups + g) * LANES, (kk * groups + g + 1) * LANES)
                q = q_ref[:, sl].astype(BF16)
                s = lax.dot_general(q, keys[kk], (((1,), (1,)), ((), ())), preferred_element_type=F32)
                if window:
                    s = s + bias_ref[...]
                sink = sink_ref[0, (kvh0 + kk) * groups + g] * LOG2E
                if bounded:
                    p = jnp.exp2(s)
                    denom = jnp.sum(p, axis=-1, keepdims=True) + jnp.exp2(jnp.full((1, 1), sink, F32))
                else:
                    m = jnp.maximum(jnp.max(s, axis=-1, keepdims=True), sink)
                    p = jnp.exp2(s - m)
                    denom = jnp.sum(p, axis=-1, keepdims=True) + jnp.exp2(sink - m)
                o = jnp.dot(p.astype(BF16), vals[kk], preferred_element_type=F32)
                o_ref[:, sl] = (o / denom).astype(o_ref.dtype)

    pl.when(sink_ref[0, n_heads] != 0.0)(functools.partial(heads, True))
    pl.when(sink_ref[0, n_heads] == 0.0)(functools.partial(heads, False))


def _attn_a(qkv, sink, batch, seq_len, heads, kv_heads, ctx_k=None, ctx_v=None, tq=TILES.attn_a_queries):
    m = qkv.shape[0]
    groups = heads // kv_heads
    window = ctx_k is not None
    tq = _tile(seq_len, tq)
    nb = seq_len // tq
    kps = kv_heads
    assert heads % kps == 0
    k_col, v_col = heads // kps, (heads + kv_heads) // kps
    r = tq // A_WINDOW
    rows128 = seq_len // A_WINDOW

    def main_rows(b, i):
        return b * nb + i

    def prev_rows(b, i):
        return b * rows128 + jnp.maximum(i * r - 1, 0)

    def next_rows(b, i):
        return b * rows128 + jnp.minimum((i + 1) * r, rows128 - 1)

    sink_spec = pl.BlockSpec(memory_space=pltpu.SMEM)
    q_spec = pl.BlockSpec((tq, kps * groups * LANES), lambda b, i, k: (main_rows(b, i), k))

    def kv_specs(col):
        main = pl.BlockSpec((tq, kps * LANES), lambda b, i, k: (main_rows(b, i), col + k))
        if not window:
            return [main]
        return [pl.BlockSpec((A_WINDOW, kps * LANES), lambda b, i, k: (prev_rows(b, i), col + k)),
                main,
                pl.BlockSpec((A_WINDOW, kps * LANES), lambda b, i, k: (next_rows(b, i), col + k)),
                pl.BlockSpec((1, ctx_k.shape[1], kps * LANES), lambda b, i, k: (b, 0, k))]

    lead_specs = [sink_spec]
    if window:
        bias = _window_bias(tq, ctx_k.shape[1])
        lead_specs.append(pl.BlockSpec((None,) + bias.shape[1:],
                                       lambda b, i, k: (jnp.where(i == 0, 1, 0) + jnp.where(i == nb - 1, 2, 0), 0, 0)))
        args = [sink, bias, qkv, qkv, qkv, qkv, ctx_k, qkv, qkv, qkv, ctx_v]
    else:
        args = [sink, qkv, qkv, qkv]
    return pl.pallas_call(
        functools.partial(_attn_a_kernel, groups=groups, window=window, kv_per_step=kps),
        grid=(batch, nb, kv_heads // kps),
        in_specs=lead_specs + [q_spec] + kv_specs(k_col) + kv_specs(v_col),
        out_specs=pl.BlockSpec((tq, kps * groups * LANES), lambda b, i, k: (main_rows(b, i), k)),
        out_shape=jax.ShapeDtypeStruct((m, heads * LANES), BF16),
        compiler_params=_params("parallel", "arbitrary", "arbitrary"),
        name="attn_a_window" if window else "attn_a_ctx",
    )(*args)


SCORE_BOUND = 40.0


def _scores_bounded(q_gain, k_gain, head_dim, ctx_k=None):
    q_norm = math.sqrt(head_dim) * jnp.max(jnp.abs(q_gain))
    k_norm = math.sqrt(head_dim) * jnp.max(jnp.abs(k_gain))
    if ctx_k is not None:
        k_norm = jnp.maximum(k_norm, jnp.sqrt(jnp.max(jnp.sum(jnp.square(ctx_k.astype(F32)), axis=-1))))
    return (q_norm * k_norm <= SCORE_BOUND).astype(F32)


def _lane_group_sum(p):
    return functools.reduce(jnp.add, [p[:, t * LANES:(t + 1) * LANES] for t in range(p.shape[1] // LANES)])


def _attn_b_kernel(*refs, has_ctx, out_scale, ck):
    if has_ctx:
        scal_ref, q_ref, k_ref, v_ref, kc_ref, vc_ref, sg_ref, o_ref = refs
    else:
        scal_ref, q_ref, k_ref, v_ref, sg_ref, o_ref = refs
    chunks =[(k_ref, v_ref, s0) for s0 in range(0, k_ref.shape[0], ck)]
    if has_ctx:
        chunks += [(kc_ref.at[0], vc_ref.at[0], s0) for s0 in range(0, kc_ref.shape[1], ck)]

    hd = 2 * LANES
    n_heads = o_ref.shape[1] // hd

    def scores(col, kr, s0):
        q = q_ref[:, col:col + LANES].astype(BF16)
        k = kr[s0:s0 + ck, col:col + LANES].astype(BF16)
        return lax.dot_general(q, k, (((1,), (1,)), ((), ())), preferred_element_type=F32)

    def finish(base, state):
        outs = [acc / jnp.sum(l, axis=-1, keepdims=True) for l, acc in state]
        o = outs[0] - scal_ref[0, 0] * outs[1]
        o_ref[:, base:base + hd] = (_rms(o) * sg_ref[...] * out_scale).astype(o_ref.dtype)

    def bounded_head(base):
        state = [None, None]
        for kr, vr, s0 in chunks:
            v = vr[s0:s0 + ck, base:base + hd].astype(BF16)
            for c in range(2):
                p = jnp.exp2(scores(base + c * LANES, kr, s0))
                l, acc = _lane_group_sum(p), jnp.dot(p.astype(BF16), v, preferred_element_type=F32)
                state[c] = (l, acc) if state[c] is None else (state[c][0] + l, state[c][1] + acc)
        finish(base, state)

    def online_head(base):
        state = [None, None]
        for kr, vr, s0 in chunks:
            v = vr[s0:s0 + ck, base:base + hd].astype(BF16)
            for c in range(2):
                s = scores(base + c * LANES, kr, s0)
                row_max = jnp.max(s, axis=-1, keepdims=True)
                if state[c] is None:
                    p = jnp.exp2(s - row_max)
                    state[c] = (row_max, _lane_group_sum(p), jnp.dot(p.astype(BF16), v, preferred_element_type=F32))
                else:
                    m_old, l_old, acc_old = state[c]
                    m_new = jnp.maximum(m_old, row_max)
                    alpha = jnp.exp2(m_old - m_new)
                    p = jnp.exp2(s - m_new)
                    state[c] = (m_new, alpha * l_old + _lane_group_sum(p),
                                alpha * acc_old + jnp.dot(p.astype(BF16), v, preferred_element_type=F32))
        finish(base, [(l, acc) for _, l, acc in state])

    @pl.when(scal_ref[0, 1] != 0.0)
    def _():
        for hh in range(n_heads):
            bounded_head(hh * hd)

    @pl.when(scal_ref[0, 1] == 0.0)
    def _():
        for hh in range(n_heads):
            online_head(hh * hd)


def _attn_b(qkv, scal, subln_g, out_scale, batch, seq_len, heads, ctx_k=None, ctx_v=None,
            tq=TILES.attn_b_queries, ck=TILES.attn_b_keys):
    m = qkv.shape[0]
    hd = 2 * LANES
    has_ctx = ctx_k is not None
    tq = _tile(seq_len, tq)
    ck = _tile(seq_len, ck, *([ctx_k.shape[1]] if has_ctx else []))
    nb = seq_len // tq
    hps = 1 if has_ctx else heads
    wd = hps * hd
    n_hblk = heads // hps
    in_specs = [pl.BlockSpec(memory_space=pltpu.SMEM),
                pl.BlockSpec((tq, wd), lambda b, h, i: (b * nb + i, h)),
                pl.BlockSpec((seq_len, wd), lambda b, h, i: (b, n_hblk + h)),
                pl.BlockSpec((seq_len, wd), lambda b, h, i: (b, 2 * n_hblk + h))]
    args = [scal, qkv, qkv, qkv]
    if has_ctx:
        in_specs += [pl.BlockSpec((1, ctx_k.shape[1], hd), lambda b, h, i: (b, 0, h))] * 2
        args += [ctx_k, ctx_v]
    in_specs.append(pl.BlockSpec((1, hd), lambda b, h, i: (0, 0)))
    args.append(subln_g)
    return pl.pallas_call(
        functools.partial(_attn_b_kernel, has_ctx=has_ctx, out_scale=out_scale, ck=ck),
        grid=(batch, n_hblk, nb),
        in_specs=in_specs,
        out_specs=pl.BlockSpec((tq, wd), lambda b, h, i: (b * nb + i, h)),
        out_shape=jax.ShapeDtypeStruct((m, heads * hd), BF16),
        compiler_params=_params("parallel", "arbitrary", "arbitrary"),
        name="attn_b_latent" if has_ctx else "attn_b_ctx",
    )(*args)


def _proj_residual_kernel(a_ref, w_ref, x_ref, gate_ref, o_ref):
    y = jnp.dot(a_ref[...], w_ref[...], preferred_element_type=F32)
    o_ref[...] = x_ref[...] + gate_ref[0] * y


def _proj_residual(a, w, layer, x, gate, seq_len, tm, tn):
    m, k = a.shape
    n = w.shape[2]
    per_batch = gate.shape[0] > 1
    tm = _tile(m, tm, seq_len) if per_batch else _tile(m, tm)
    tn = _tile(n, tn)
    mod_idx = _mod_index(per_batch, max(seq_len // tm, 1))
    gate_idx = lambda i, j: mod_idx(i, j)[:2] + (j,)
    return pl.pallas_call(
        _proj_residual_kernel,
        grid=(m // tm, n // tn),
        in_specs=[pl.BlockSpec((tm, k), lambda i, j: (i, 0)),
                  pl.BlockSpec((None, k, tn), lambda i, j: (layer, 0, j)),
                  pl.BlockSpec((tm, tn), lambda i, j: (i, j)),
                  pl.BlockSpec((1, 1, tn), gate_idx)],
        out_specs=pl.BlockSpec((tm, tn), lambda i, j: (i, j)),
        out_shape=jax.ShapeDtypeStruct((m, n), F32),
        compiler_params=_params("parallel", "arbitrary"),
        name="proj_residual",
    )(a, w, x, gate)


HALO = BF16_ROWS


def _ffn_kernel(xp_ref, x_ref, xn_ref, g_ref, sh_ref, sc_ref, gate_ref, wg_ref, wv_ref, cw_ref, cb_ref, wd_ref,
                o_ref, h_ref, inv_ref, *, seq_len, tm):
    i = pl.program_id(0)
    f = pl.program_id(1)

    @pl.when(f == 0)
    def _():
        inv_ref[...] = lax.rsqrt(jnp.mean(jnp.square(x_ref[...]), axis=-1, keepdims=True) + EPS)

    @pl.when(f == 0)
    def _():
        g, sh, sc = g_ref[...], sh_ref[0], sc_ref[0]
        h_ref[0:HALO] = _adaln(xp_ref[...], g, sh, sc).astype(BF16)
        h_ref[HALO:HALO + tm] = (x_ref[...] * inv_ref[...] * (g * (1.0 + sc)) + sh).astype(BF16)
        h_ref[HALO + tm:] = _adaln(xn_ref[...], g, sh, sc).astype(BF16)
        o_ref[...] = jnp.zeros_like(o_ref)

    gate_pre = jnp.dot(h_ref[...], wg_ref[...], preferred_element_type=F32)
    val = jnp.dot(h_ref[HALO:HALO + tm], wv_ref[...], preferred_element_type=F32)
    pos = (i * tm + lax.broadcasted_iota(jnp.int32, (tm, 1), 0)) & (seq_len - 1)
    up = jnp.where(pos != 0, gate_pre[HALO - 1:HALO - 1 + tm], 0.0)
    dn = jnp.where(pos != seq_len - 1, gate_pre[HALO + 1:HALO + 1 + tm], 0.0)
    gc = up * cw_ref[0:1] + gate_pre[HALO:HALO + tm] * cw_ref[1:2] + dn * cw_ref[2:3] + cb_ref[...]
    act = (gc / (1.0 + jnp.exp(-gc)) * val).astype(BF16)
    o_ref[...] += jnp.dot(act, wd_ref[...], preferred_element_type=F32)

    @pl.when(f == pl.num_programs(1) - 1)
    def _():
        o_ref[...] = x_ref[...] + gate_ref[0] * o_ref[...]


def _ffn(x, g, shift, scale, gate, w_up, conv_w, conv_b, w_down, layer, seq_len, tm, tf):
    m, d = x.shape
    ff = w_down.shape[1]
    per_batch = shift.shape[0] > 1
    tm = _tile(m, tm, seq_len) if per_batch else _tile(m, tm)
    tf = _tile(ff, tf)
    assert (seq_len % tm == 0 or tm % seq_len == 0) and tm % HALO == 0
    assert seq_len & (seq_len - 1) == 0, "token position uses a power-of-two mask"
    mod_idx = _mod_index(per_batch, max(seq_len // tm, 1))
    hb = tm // HALO
    n_halo_blocks = m // HALO
    nf = ff // tf
    return pl.pallas_call(
        functools.partial(_ffn_kernel, seq_len=seq_len, tm=tm),
        grid=(m // tm, nf),
        in_specs=[pl.BlockSpec((HALO, d), lambda i, f: (jnp.maximum(i * hb - 1, 0), 0)),
                  pl.BlockSpec((tm, d), lambda i, f: (i, 0)),
                  pl.BlockSpec((HALO, d), lambda i, f: (jnp.minimum((i + 1) * hb, n_halo_blocks - 1), 0)),
                  pl.BlockSpec((1, d), lambda i, f: (0, 0)),
                  pl.BlockSpec((1, 1, d), mod_idx),
                  pl.BlockSpec((1, 1, d), mod_idx),
                  pl.BlockSpec((1, 1, d), mod_idx),
                  pl.BlockSpec((None, d, tf), lambda i, f: (layer, 0, f)),
                  pl.BlockSpec((None, d, tf), lambda i, f: (layer, 0, nf + f)),
                  pl.BlockSpec((3, tf), lambda i, f: (0, f)),
                  pl.BlockSpec((1, tf), lambda i, f: (0, f)),
                  pl.BlockSpec((None, tf, d), lambda i, f: (layer, f, 0))],
        out_specs=pl.BlockSpec((tm, d), lambda i, f: (i, 0)),
        out_shape=jax.ShapeDtypeStruct((m, d), F32),
        scratch_shapes=[pltpu.VMEM((tm + 2 * HALO, d), BF16), pltpu.VMEM((tm, 1), F32)],
        compiler_params=_params("parallel", "arbitrary"),
        name="conv_glu",
    )(x, x, x, g, shift, scale, gate, w_up, w_up, conv_w, conv_b, w_down)


def _rope_tables(n, dim):
    rows = n // GRID_W
    row = jnp.repeat(jnp.arange(rows, dtype=F32), GRID_W)
    col = jnp.tile(jnp.arange(GRID_W, dtype=F32), rows)
    half = dim // 2
    inv = ROPE_BASE ** (-jnp.arange(0, half, 2, dtype=F32) / half)
    ar, ac = row[:, None] * inv, col[:, None] * inv
    cos = jnp.concatenate([jnp.cos(ar), jnp.cos(ar), jnp.cos(ac), jnp.cos(ac)], axis=-1)
    sin = jnp.concatenate([-jnp.sin(ar), jnp.sin(ar), -jnp.sin(ac), jnp.sin(ac)], axis=-1)
    reps = MXU_COLS // dim
    return jnp.tile(cos, (1, reps)), jnp.tile(sin, (1, reps))


def kernel(x_prompt, x_sample, cache_a_k, cache_a_v, cache_b_k, cache_b_v, c, c_ctx, ada_w, ada_b, norm1_g, norm2_g, a_w_qkv, a_q_norm, a_k_norm, a_sink, a_w_o, b_w_qkv, b_q_norm, b_k_norm, b_lambda_q1, b_lambda_k1, b_lambda_q2, b_lambda_k2, b_subln, b_w_o, ffn_w_up, ffn_conv_w, ffn_conv_b, ffn_w_down):
    batch, seq, d = x_prompt.shape
    dec_batch, dec_seq, _ = x_sample.shape
    depth = ada_w.shape[0]
    a_heads = a_sink.shape[1]
    a_kvh, a_hd = cache_a_k.shape[3], cache_a_k.shape[4]
    b_heads, b_qk = cache_b_k.shape[3], cache_b_k.shape[5]
    b_vd = cache_b_v.shape[4]
    past = cache_a_k.shape[2]
    assert a_hd == LANES and b_qk == LANES and b_vd == 2 * LANES

    n_cond = -(-(dec_batch + 1) // BF16_ROWS) * BF16_ROWS
    cond = jnp.concatenate([c, c_ctx[None, :], jnp.zeros((n_cond - dec_batch - 1, d), F32)], axis=0)
    mod = _modulation(cond, ada_w, ada_b).reshape(depth, n_cond, 6, d)
    cos, sin = _rope_tables(dec_seq, LANES)

    xp = x_prompt.reshape(batch * seq, d)
    xs = x_sample.reshape(dec_batch * dec_seq, d)
    n_a, n_b = (depth + 1) // 2, depth // 2
    a_k_state = a_v_state = b_k_state = b_v_state = None
    a_wqkv, a_wo, b_wqkv, b_wo, w_up, w_down = [w.astype(BF16) for w in
                                               (a_w_qkv, a_w_o, b_w_qkv, b_w_o, ffn_w_up, ffn_w_down)]

    for l in range(depth):
        mod_s = [mod[l, :dec_batch, t][:, None, :] for t in range(6)]
        mod_p = [mod[l, dec_batch:dec_batch + 1, t][:, None, :] for t in range(6)]
        n1 = norm1_g[l][None, :]
        n2 = norm2_g[l][None, :]
        j = l // 2
        if l % 2 == 0:
            w_qkv, w_o = a_wqkv, a_wo
            qn, kn = a_heads * a_hd, a_kvh * a_hd
            q_gain = a_q_norm[j] * (a_hd ** -0.5 * LOG2E)
            head_gain = jnp.concatenate([jnp.tile(q_gain, a_heads),
                                         jnp.tile(a_k_norm[j], a_kvh), jnp.ones((kn,), F32)])[None, :]
            sink_ok = (jnp.max(jnp.abs(a_sink[j])) * LOG2E <= SCORE_BOUND).astype(F32)
            sink_p = jnp.concatenate([a_sink[j], sink_ok * _scores_bounded(q_gain, a_k_norm[j], a_hd)[None]])[None, :]
            sink_s = jnp.concatenate([a_sink[j], sink_ok * _scores_bounded(q_gain, a_k_norm[j], a_hd,
                                                                          cache_a_k[:, j])[None]])[None, :]
            qkv_p, a_k_state, a_v_state = _qkv(
                xp, n1, mod_p[0], mod_p[1], w_qkv, j, head_gain, qn + kn, seq, None, BF16,
                tm=TILES.qkv_rows, tn=TILES.qkv_cols_context,
                state=_KvState(a_k_state, a_v_state, n_a, j, qn, kn, qn + kn, kn))
            op = _attn_a(qkv_p, sink_p, batch, seq, a_heads, a_kvh)
            qkv_s = _qkv(xs, n1, mod_s[0], mod_s[1], w_qkv, j, head_gain, qn + kn, dec_seq, (cos, sin), BF16,
                         tm=TILES.qkv_rows, tn=TILES.qkv_cols_latent)
            ctx_k = cache_a_k[:, j].reshape(dec_batch, past, kn).astype(BF16)
            ctx_v = cache_a_v[:, j].reshape(dec_batch, past, kn).astype(BF16)
            os_ = _attn_a(qkv_s, sink_s, dec_batch, dec_seq, a_heads, a_kvh, ctx_k, ctx_v)
        else:
            w_qkv, w_o = b_wqkv, b_wo
            lambda_init = 0.8 - 0.6 * math.exp(-0.3 * l)
            lam = (jnp.exp(jnp.sum(b_lambda_q1[j] * b_lambda_k1[j])) - jnp.exp(jnp.sum(b_lambda_q2[j] * b_lambda_k2[j]))
                   + lambda_init)
            qn = b_heads * 2 * b_qk
            q_gain = b_q_norm[j] * (b_qk ** -0.5 * LOG2E)
            head_gain = jnp.concatenate([jnp.tile(q_gain, 2 * b_heads),
                                         jnp.tile(b_k_norm[j], 2 * b_heads), jnp.ones((qn,), F32)])[None, :]
            subln = b_subln[j][None, :]
            scal_p = jnp.stack([lam, _scores_bounded(q_gain, b_k_norm[j], b_qk)])[None, :]
            scal_s = jnp.stack([lam, _scores_bounded(q_gain, b_k_norm[j], b_qk, cache_b_k[:, j])])[None, :]
            qkv_p, b_k_state, b_v_state = _qkv(
                xp, n1, mod_p[0], mod_p[1], w_qkv, j, head_gain, 2 * qn, seq, None, BF16,
                tm=TILES.qkv_rows, tn=TILES.qkv_cols_context,
                state=_KvState(b_k_state, b_v_state, n_b, j, qn, qn, 2 * qn, qn))
            op = _attn_b(qkv_p, scal_p, subln, 1.0 - lambda_init, batch, seq, b_heads)
            qkv_s = _qkv(xs, n1, mod_s[0], mod_s[1], w_qkv, j, head_gain, 2 * qn, dec_seq, (cos, sin), BF16,
                         tm=TILES.qkv_rows, tn=TILES.qkv_cols_latent)
            ctx_k = cache_b_k[:, j].reshape(dec_batch, past, qn).astype(BF16)
            ctx_v = cache_b_v[:, j].reshape(dec_batch, past, qn).astype(BF16)
            os_ = _attn_b(qkv_s, scal_s, subln, 1.0 - lambda_init, dec_batch, dec_seq, b_heads, ctx_k, ctx_v)
        xp = _proj_residual(op, w_o, j, xp, mod_p[2], seq, tm=TILES.proj_rows, tn=TILES.proj_cols)
        xs = _proj_residual(os_, w_o, j, xs, mod_s[2], dec_seq, tm=TILES.proj_rows, tn=TILES.proj_cols)
        cb = ffn_conv_b[l][None, :]
        xp = _ffn(xp, n2, mod_p[3], mod_p[4], mod_p[5], w_up, ffn_conv_w[l], cb, w_down, l, seq,
                  tm=TILES.ffn_rows, tf=TILES.ffn_cols)
        xs = _ffn(xs, n2, mod_s[3], mod_s[4], mod_s[5], w_up, ffn_conv_w[l], cb, w_down, l, dec_seq,
                  tm=TILES.ffn_rows, tf=TILES.ffn_cols)

    return (xp.reshape(batch, seq, d), xs.reshape(dec_batch, dec_seq, d),
            a_k_state.reshape(batch, n_a, seq, a_kvh, a_hd), a_v_state.reshape(batch, n_a, seq, a_kvh, a_hd),
            b_k_state.reshape(batch, n_b, seq, b_heads, 2, b_qk), b_v_state.reshape(batch, n_b, seq, b_heads, b_vd))
```

```python
import functools
import math
from typing import NamedTuple, Optional

import jax
import jax.numpy as jnp
from jax import lax
from jax.experimental import pallas as pl
from jax.experimental.pallas import tpu as pltpu

GRID_W = 64
A_WINDOW = 128
ROPE_BASE = 10000.0
EPS = 1e-6
NEG_INF = -1e30
LANES = 128
MXU_COLS = 256
BF16_ROWS = 16
LOG2E = math.log2(math.e)
VMEM_LIMIT = 56 * 1024 * 1024

BF16 = jnp.bfloat16
F32 = jnp.float32


class _Tiles(NamedTuple):
    modulation_cols: int = 1024
    qkv_rows: int = 1024
    qkv_cols_latent: int = 1536
    qkv_cols_context: int = 1024
    proj_rows: int = 512
    proj_cols: int = 2048
    ffn_rows: int = 1024
    ffn_cols: int = 512
    attn_a_queries: int = 256
    attn_b_queries: int = 1024
    attn_b_keys: int = 512


TILES = _Tiles()


def _params(*sem):
    return pltpu.CompilerParams(dimension_semantics=sem, vmem_limit_bytes=VMEM_LIMIT)


def _tile(size, target, *also):
    t = min(target, size)
    while any(s % t for s in (size,) + also):
        t //= 2
    return t


def _rms(x):
    return x * lax.rsqrt(jnp.mean(x * x, axis=-1, keepdims=True) + EPS)


def _adaln(x, g, shift, scale):
    return _rms(x) * g * (1.0 + scale) + shift


def _mod_index(per_batch, tiles_per_seq):
    if per_batch:
        return lambda i, j: (i // tiles_per_seq, 0, 0)
    return lambda i, j: (0, 0, 0)


def _modulation_kernel(c_ref, w_ref, b_ref, o_ref):
    c = c_ref[...]
    s = (c / (1.0 + jnp.exp(-c))).astype(BF16)
    o_ref[0] = jnp.dot(s, w_ref[0].astype(BF16), preferred_element_type=F32) + b_ref[0]


def _modulation(cond, ada_w, ada_b, tn=TILES.modulation_cols):
    depth, d, n = ada_w.shape
    r = cond.shape[0]
    return pl.pallas_call(
        _modulation_kernel,
        grid=(depth, n // tn),
        in_specs=[pl.BlockSpec((r, d), lambda l, j: (0, 0)),
                  pl.BlockSpec((1, d, tn), lambda l, j: (l, 0, j)),
                  pl.BlockSpec((1, 1, tn), lambda l, j: (l, 0, j))],
        out_specs=pl.BlockSpec((1, r, tn), lambda l, j: (l, 0, j)),
        out_shape=jax.ShapeDtypeStruct((depth, r, n), F32),
        compiler_params=_params("arbitrary", "arbitrary"),
        name="modulation",
    )(cond, ada_w, ada_b.reshape(depth, 1, n))


def _head_matrices():
    i = jnp.arange(MXU_COLS)
    mean = jnp.where(i[:, None] // LANES == i[None, :] // LANES, 1.0 / LANES, 0.0)
    quarter = LANES // 4
    partner = jnp.where(i % (2 * quarter) < quarter, i + quarter, i - quarter)
    perm = (i[:, None] == partner[None, :])
    return mean.astype(BF16), perm.astype(BF16)


class _KvState(NamedTuple):
    k_prev: Optional[jax.Array]
    v_prev: Optional[jax.Array]
    slots: int
    slot: int
    k_start: int
    k_width: int
    v_start: int
    v_width: int


def _qkv_kernel(*refs, n_norm_cols, n_tiles, rope, state, n_aliased):
    n_in = 7 + (3 if rope else 0) + n_aliased
    x_ref, g_ref, sh_ref, sc_ref, w_ref, hg_ref, mean_ref = refs[:7]
    if rope:
        perm_ref, cos_ref, sin_ref = refs[7:10]
    if state is not None:
        o_ref, ks_ref, vs_ref, h_ref, inv_ref = refs[n_in:]
    else:
        o_ref, h_ref, inv_ref = refs[n_in:]
    j = pl.program_id(1)

    @pl.when(j == 0)
    def _():
        inv_ref[...] = lax.rsqrt(jnp.mean(jnp.square(x_ref[...]), axis=-1, keepdims=True) + EPS)

    @pl.when(j == 0)
    def _():
        gain = g_ref[...] * (1.0 + sc_ref[0])
        h_ref[...] = (x_ref[...] * inv_ref[...] * gain + sh_ref[0]).astype(BF16)

    tm, tn = o_ref.shape
    subs = tn // MXU_COLS
    halves = 2 if tm % (2 * BF16_ROWS) == 0 and (state is None or ks_ref.shape[0] % 2 == 0) else 1
    hm = tm // halves

    def tile_body(tile):
        for r in range(halves):
            rows = slice(r * hm, (r + 1) * hm)
            y_all = jnp.dot(h_ref[rows], w_ref[...], preferred_element_type=F32)
            for c in range(subs):
                col = tile * tn + c * MXU_COLS
                sl = slice(c * MXU_COLS, (c + 1) * MXU_COLS)
                y = y_all[:, sl]
                if col < n_norm_cols:
                    ms = jnp.dot((y * y).astype(BF16), mean_ref[...], preferred_element_type=F32)
                    z = y * hg_ref[:, sl]
                    if rope:
                        partner = jnp.dot(z.astype(BF16), perm_ref[...], preferred_element_type=F32)
                        z = z * cos_ref[rows] + partner * sin_ref[rows]
                    y = z * lax.rsqrt(ms + EPS)
                o_ref[rows, sl] = y.astype(o_ref.dtype)
                if state is not None:
                    for ref, start, width in ((ks_ref, state.k_start, state.k_width),
                                              (vs_ref, state.v_start, state.v_width)):
                        if start <= col < start + width:
                            off = (col - start) % ref.shape[2]
                            sq = ref.shape[0] // halves
                            ref[r * sq:(r + 1) * sq, :, off:off + MXU_COLS] = y.reshape(sq, ref.shape[1], MXU_COLS)

    for tile in range(n_tiles):
        pl.when(j == tile)(functools.partial(tile_body, tile))


def _qkv(x, g, shift, scale, w, layer, head_gain, n_norm_cols, seq_len, rope_tabs, out_dtype, tm, tn, state=None):
    m, d = x.shape
    n = w.shape[2]
    per_batch = shift.shape[0] > 1
    tn = _tile(n, tn)
    assert tn % MXU_COLS == 0 and n_norm_cols % MXU_COLS == 0
    tm = _tile(m, tm, seq_len) if (per_batch or rope_tabs is not None) else _tile(m, tm)
    tiles_per_seq = max(seq_len // tm, 1)
    mod_idx = _mod_index(per_batch, tiles_per_seq)
    mean_mat, perm_mat = _head_matrices()
    const_spec = pl.BlockSpec((MXU_COLS, MXU_COLS), lambda i, j: (0, 0))
    in_specs = [pl.BlockSpec((tm, d), lambda i, j: (i, 0)),
                pl.BlockSpec((1, d), lambda i, j: (0, 0)),
                pl.BlockSpec((1, 1, d), mod_idx),
                pl.BlockSpec((1, 1, d), mod_idx),
                pl.BlockSpec((None, d, tn), lambda i, j: (layer, 0, j)),
                pl.BlockSpec((1, tn), lambda i, j: (0, j)),
                const_spec]
    args = [x, g, shift, scale, w, head_gain, mean_mat]
    if rope_tabs is not None:
        in_specs += [const_spec] + [pl.BlockSpec((tm, MXU_COLS), lambda i, j: (i % tiles_per_seq, 0))] * 2
        args += [perm_mat] + list(rope_tabs)
    out_specs = [pl.BlockSpec((tm, tn), lambda i, j: (i, j))]
    out_shape = [jax.ShapeDtypeStruct((m, n), out_dtype)]
    aliases = {}
    if state is not None:
        assert tm % seq_len == 0 and not per_batch
        seqs = tm // seq_len
        for start, width, prev in ((state.k_start, state.k_width, state.k_prev),
                                   (state.v_start, state.v_width, state.v_prev)):
            bw = min(tn, width)
            assert width % bw == 0 and start % MXU_COLS == 0 and (start % tn) % bw == 0
            idx = (lambda i, j, start=start, bw=bw, nblk=width // bw:
                   (i, state.slot, 0, jnp.clip((j * tn - start) // bw, 0, nblk - 1)))
            out_specs.append(pl.BlockSpec((seqs, None, seq_len, bw), idx))
            out_shape.append(jax.ShapeDtypeStruct((m // seq_len, state.slots, seq_len, width), F32))
            if prev is not None:
                aliases[len(args)] = len(out_shape) - 1
                in_specs.append(pl.BlockSpec(memory_space=pl.ANY))
                args.append(prev)
    out = pl.pallas_call(
        functools.partial(_qkv_kernel, n_norm_cols=n_norm_cols, n_tiles=n // tn, rope=rope_tabs is not None,
                          state=None if state is None else state._replace(k_prev=None, v_prev=None),
                          n_aliased=len(aliases)),
        grid=(m // tm, n // tn),
        in_specs=in_specs,
        out_specs=out_specs,
        out_shape=out_shape,
        input_output_aliases=aliases,
        scratch_shapes=[pltpu.VMEM((tm, d), BF16), pltpu.VMEM((tm, 1), F32)],
        compiler_params=_params("parallel", "arbitrary"),
        name="adaln_qkv",
    )(*args)
    return out if state is not None else out[0]


def _window_bias(tq, n_ctx):
    n_lat = tq + 2 * A_WINDOW
    qi = jnp.arange(tq)[:, None]
    kj = jnp.arange(n_lat + n_ctx)[None, :]
    band = jnp.abs(kj - A_WINDOW - qi) <= A_WINDOW
    variants = []
    for last in (False, True):
        for first in (False, True):
            in_seq = ((kj >= A_WINDOW) if first else True) & ((kj < tq + A_WINDOW) if last else True)
            variants.append(jnp.where((kj >= n_lat) | (band & in_seq), 0.0, NEG_INF))
    return jnp.stack(variants).astype(F32)


def _attn_a_kernel(*refs, groups, window, kv_per_step):
    if window:
        (sink_ref, bias_ref, q_ref, kp_ref, km_ref, kn_ref, kc_ref, vp_ref, vm_ref, vn_ref, vc_ref, o_ref) = refs
    else:
        sink_ref, q_ref, km_ref, vm_ref, o_ref = refs
    kvh0 = pl.program_id(2) * kv_per_step
    def head_rows(refs_, kk):
        cols = slice(kk * LANES, (kk + 1) * LANES)
        pieces = [r[0, :, cols] if len(r.shape) == 3 else r[:, cols] for r in refs_]
        return pieces[0].astype(BF16) if len(pieces) == 1 else jnp.concatenate(pieces, axis=0)

    k_refs = (kp_ref, km_ref, kn_ref, kc_ref) if window else (km_ref,)
    v_refs = (vp_ref, vm_ref, vn_ref, vc_ref) if window else (vm_ref,)
    keys = [head_rows(k_refs, kk) for kk in range(kv_per_step)]
    vals = [head_rows(v_refs, kk) for kk in range(kv_per_step)]
    n_heads = sink_ref.shape[1] - 1

    def heads(bounded):
        for kk in range(kv_per_step):
            for g in range(groups):
                sl = slice((kk * groups + g) * LANES, (kk * groups + g + 1) * LANES)
                q = q_ref[:, sl].astype(BF16)
                s = lax.dot_general(q, keys[kk], (((1,), (1,)), ((), ())), preferred_element_type=F32)
                if window:
                    s = s + bias_ref[...]
                sink = sink_ref[0, (kvh0 + kk) * groups + g] * LOG2E
                if bounded:
                    p = jnp.exp2(s)
                    denom = jnp.sum(p, axis=-1, keepdims=True) + jnp.exp2(jnp.full((1, 1), sink, F32))
                else:
                    m = jnp.maximum(jnp.max(s, axis=-1, keepdims=True), sink)
                    p = jnp.exp2(s - m)
                    denom = jnp.sum(p, axis=-1, keepdims=True) + jnp.exp2(sink - m)
                o = jnp.dot(p.astype(BF16), vals[kk], preferred_element_type=F32)
                o_ref[:, sl] = (o / denom).astype(o_ref.dtype)

    pl.when(sink_ref[0, n_heads] != 0.0)(functools.partial(heads, True))
    pl.when(sink_ref[0, n_heads] == 0.0)(functools.partial(heads, False))


def _attn_a(qkv, sink, batch, seq_len, heads, kv_heads, ctx_k=None, ctx_v=None, tq=TILES.attn_a_queries):
    m = qkv.shape[0]
    groups = heads // kv_heads
    window = ctx_k is not None
    tq = _tile(seq_len, tq)
    nb = seq_len // tq
    kps = kv_heads
    assert heads % kps == 0
    k_col, v_col = heads // kps, (heads + kv_heads) // kps
    r = tq // A_WINDOW
    rows128 = seq_len // A_WINDOW

    def main_rows(b, i):
        return b * nb + i

    def prev_rows(b, i):
        return b * rows128 + jnp.maximum(i * r - 1, 0)

    def next_rows(b, i):
        return b * rows128 + jnp.minimum((i + 1) * r, rows128 - 1)

    sink_spec = pl.BlockSpec(memory_space=pltpu.SMEM)
    q_spec = pl.BlockSpec((tq, kps * groups * LANES), lambda b, i, k: (main_rows(b, i), k))

    def kv_specs(col):
        main = pl.BlockSpec((tq, kps * LANES), lambda b, i, k: (main_rows(b, i), col + k))
        if not window:
            return [main]
        return [pl.BlockSpec((A_WINDOW, kps * LANES), lambda b, i, k: (prev_rows(b, i), col + k)),
                main,
                pl.BlockSpec((A_WINDOW, kps * LANES), lambda b, i, k: (next_rows(b, i), col + k)),
                pl.BlockSpec((1, ctx_k.shape[1], kps * LANES), lambda b, i, k: (b, 0, k))]

    lead_specs = [sink_spec]
    if window:
        bias = _window_bias(tq, ctx_k.shape[1])
        lead_specs.append(pl.BlockSpec((None,) + bias.shape[1:],
                                       lambda b, i, k: (jnp.where(i == 0, 1, 0) + jnp.where(i == nb - 1, 2, 0), 0, 0)))
        args = [sink, bias, qkv, qkv, qkv, qkv, ctx_k, qkv, qkv, qkv, ctx_v]
    else:
        args = [sink, qkv, qkv, qkv]
    return pl.pallas_call(
        functools.partial(_attn_a_kernel, groups=groups, window=window, kv_per_step=kps),
        grid=(batch, nb, kv_heads // kps),
        in_specs=lead_specs + [q_spec] + kv_specs(k_col) + kv_specs(v_col),
        out_specs=pl.BlockSpec((tq, kps * groups * LANES), lambda b, i, k: (main_rows(b, i), k)),
        out_shape=jax.ShapeDtypeStruct((m, heads * LANES), BF16),
        compiler_params=_params("parallel", "arbitrary", "arbitrary"),
        name="attn_a_window" if window else "attn_a_ctx",
    )(*args)


SCORE_BOUND = 40.0


def _scores_bounded(q_gain, k_gain, head_dim, ctx_k=None):
    q_norm = math.sqrt(head_dim) * jnp.max(jnp.abs(q_gain))
    k_norm = math.sqrt(head_dim) * jnp.max(jnp.abs(k_gain))
    if ctx_k is not None:
        k_norm = jnp.maximum(k_norm, jnp.sqrt(jnp.max(jnp.sum(jnp.square(ctx_k.astype(F32)), axis=-1))))
    return (q_norm * k_norm <= SCORE_BOUND).astype(F32)


def _lane_group_sum(p):
    return functools.reduce(jnp.add, [p[:, t * LANES:(t + 1) * LANES] for t in range(p.shape[1] // LANES)])


def _attn_b_kernel(*refs, has_ctx, out_scale, ck):
    if has_ctx:
        scal_ref, q_ref, k_ref, v_ref, kc_ref, vc_ref, sg_ref, o_ref = refs
    else:
        scal_ref, q_ref, k_ref, v_ref, sg_ref, o_ref = refs
    chunks =[(k_ref, v_ref, s0) for s0 in range(0, k_ref.shape[0], ck)]
    if has_ctx:
        chunks += [(kc_ref.at[0], vc_ref.at[0], s0) for s0 in range(0, kc_ref.shape[1], ck)]

    hd = 2 * LANES
    n_heads = o_ref.shape[1] // hd

    def scores(col, kr, s0):
        q = q_ref[:, col:col + LANES].astype(BF16)
        k = kr[s0:s0 + ck, col:col + LANES].astype(BF16)
        return lax.dot_general(q, k, (((1,), (1,)), ((), ())), preferred_element_type=F32)

    def finish(base, state):
        outs = [acc / jnp.sum(l, axis=-1, keepdims=True) for l, acc in state]
        o = outs[0] - scal_ref[0, 0] * outs[1]
        o_ref[:, base:base + hd] = (_rms(o) * sg_ref[...] * out_scale).astype(o_ref.dtype)

    def bounded_head(base):
        state = [None, None]
        for kr, vr, s0 in chunks:
            v = vr[s0:s0 + ck, base:base + hd].astype(BF16)
            for c in range(2):
                p = jnp.exp2(scores(base + c * LANES, kr, s0))
                l, acc = _lane_group_sum(p), jnp.dot(p.astype(BF16), v, preferred_element_type=F32)
                state[c] = (l, acc) if state[c] is None else (state[c][0] + l, state[c][1] + acc)
        finish(base, state)

    def online_head(base):
        state = [None, None]
        for kr, vr, s0 in chunks:
            v = vr[s0:s0 + ck, base:base + hd].astype(BF16)
            for c in range(2):
                s = scores(base + c * LANES, kr, s0)
                row_max = jnp.max(s, axis=-1, keepdims=True)
                if state[c] is None:
                    p = jnp.exp2(s - row_max)
                    state[c] = (row_max, _lane_group_sum(p), jnp.dot(p.astype(BF16), v, preferred_element_type=F32))
                else:
                    m_old, l_old, acc_old = state[c]
                    m_new = jnp.maximum(m_old, row_max)
                    alpha = jnp.exp2(m_old - m_new)
                    p = jnp.exp2(s - m_new)
                    state[c] = (m_new, alpha * l_old + _lane_group_sum(p),
                                alpha * acc_old + jnp.dot(p.astype(BF16), v, preferred_element_type=F32))
        finish(base, [(l, acc) for _, l, acc in state])

    @pl.when(scal_ref[0, 1] != 0.0)
    def _():
        for hh in range(n_heads):
            bounded_head(hh * hd)

    @pl.when(scal_ref[0, 1] == 0.0)
    def _():
        for hh in range(n_heads):
            online_head(hh * hd)


def _attn_b(qkv, scal, subln_g, out_scale, batch, seq_len, heads, ctx_k=None, ctx_v=None,
            tq=TILES.attn_b_queries, ck=TILES.attn_b_keys):
    m = qkv.shape[0]
    hd = 2 * LANES
    has_ctx = ctx_k is not None
    tq = _tile(seq_len, tq)
    ck = _tile(seq_len, ck, *([ctx_k.shape[1]] if has_ctx else []))
    nb = seq_len // tq
    hps = 1 if has_ctx else heads
    wd = hps * hd
    n_hblk = heads // hps
    in_specs = [pl.BlockSpec(memory_space=pltpu.SMEM),
                pl.BlockSpec((tq, wd), lambda b, h, i: (b * nb + i, h)),
                pl.BlockSpec((seq_len, wd), lambda b, h, i: (b, n_hblk + h)),
                pl.BlockSpec((seq_len, wd), lambda b, h, i: (b, 2 * n_hblk + h))]
    args = [scal, qkv, qkv, qkv]
    if has_ctx:
        in_specs += [pl.BlockSpec((1, ctx_k.shape[1], hd), lambda b, h, i: (b, 0, h))] * 2
        args += [ctx_k, ctx_v]
    in_specs.append(pl.BlockSpec((1, hd), lambda b, h, i: (0, 0)))
    args.append(subln_g)
    return pl.pallas_call(
        functools.partial(_attn_b_kernel, has_ctx=has_ctx, out_scale=out_scale, ck=ck),
        grid=(batch, n_hblk, nb),
        in_specs=in_specs,
        out_specs=pl.BlockSpec((tq, wd), lambda b, h, i: (b * nb + i, h)),
        out_shape=jax.ShapeDtypeStruct((m, heads * hd), BF16),
        compiler_params=_params("parallel", "arbitrary", "arbitrary"),
        name="attn_b_latent" if has_ctx else "attn_b_ctx",
    )(*args)


def _proj_residual_kernel(a_ref, w_ref, x_ref, gate_ref, o_ref):
    y = jnp.dot(a_ref[...], w_ref[...], preferred_element_type=F32)
    o_ref[...] = x_ref[...] + gate_ref[0] * y


def _proj_residual(a, w, layer, x, gate, seq_len, tm, tn):
    m, k = a.shape
    n = w.shape[2]
    per_batch = gate.shape[0] > 1
    tm = _tile(m, tm, seq_len) if per_batch else _tile(m, tm)
    tn = _tile(n, tn)
    mod_idx = _mod_index(per_batch, max(seq_len // tm, 1))
    gate_idx = lambda i, j: mod_idx(i, j)[:2] + (j,)
    return pl.pallas_call(
        _proj_residual_kernel,
        grid=(m // tm, n // tn),
        in_specs=[pl.BlockSpec((tm, k), lambda i, j: (i, 0)),
                  pl.BlockSpec((None, k, tn), lambda i, j: (layer, 0, j)),
                  pl.BlockSpec((tm, tn), lambda i, j: (i, j)),
                  pl.BlockSpec((1, 1, tn), gate_idx)],
        out_specs=pl.BlockSpec((tm, tn), lambda i, j: (i, j)),
        out_shape=jax.ShapeDtypeStruct((m, n), F32),
        compiler_params=_params("parallel", "arbitrary"),
        name="proj_residual",
    )(a, w, x, gate)


HALO = BF16_ROWS


def _ffn_kernel(xp_ref, x_ref, xn_ref, g_ref, sh_ref, sc_ref, gate_ref, wg_ref, wv_ref, cw_ref, cb_ref, wd_ref,
                o_ref, h_ref, inv_ref, *, seq_len, tm):
    i = pl.program_id(0)
    f = pl.program_id(1)

    @pl.when(f == 0)
    def _():
        inv_ref[...] = lax.rsqrt(jnp.mean(jnp.square(x_ref[...]), axis=-1, keepdims=True) + EPS)

    @pl.when(f == 0)
    def _():
        g, sh, sc = g_ref[...], sh_ref[0], sc_ref[0]
        h_ref[0:HALO] = _adaln(xp_ref[...], g, sh, sc).astype(BF16)
        h_ref[HALO:HALO + tm] = (x_ref[...] * inv_ref[...] * (g * (1.0 + sc)) + sh).astype(BF16)
        h_ref[HALO + tm:] = _adaln(xn_ref[...], g, sh, sc).astype(BF16)
        o_ref[...] = jnp.zeros_like(o_ref)

    gate_pre = jnp.dot(h_ref[...], wg_ref[...], preferred_element_type=F32)
    val = jnp.dot(h_ref[HALO:HALO + tm], wv_ref[...], preferred_element_type=F32)
    pos = (i * tm + lax.broadcasted_iota(jnp.int32, (tm, 1), 0)) & (seq_len - 1)
    up = jnp.where(pos != 0, gate_pre[HALO - 1:HALO - 1 + tm], 0.0)
    dn = jnp.where(pos != seq_len - 1, gate_pre[HALO + 1:HALO + 1 + tm], 0.0)
    gc = up * cw_ref[0:1] + gate_pre[HALO:HALO + tm] * cw_ref[1:2] + dn * cw_ref[2:3] + cb_ref[...]
    act = (gc / (1.0 + jnp.exp(-gc)) * val).astype(BF16)
    o_ref[...] += jnp.dot(act, wd_ref[...], preferred_element_type=F32)

    @pl.when(f == pl.num_programs(1) - 1)
    def _():
        o_ref[...] = x_ref[...] + gate_ref[0] * o_ref[...]


def _ffn(x, g, shift, scale, gate, w_up, conv_w, conv_b, w_down, layer, seq_len, tm, tf):
    m, d = x.shape
    ff = w_down.shape[1]
    per_batch = shift.shape[0] > 1
    tm = _tile(m, tm, seq_len) if per_batch else _tile(m, tm)
    tf = _tile(ff, tf)
    assert (seq_len % tm == 0 or tm % seq_len == 0) and tm % HALO == 0
    assert seq_len & (seq_len - 1) == 0, "token position uses a power-of-two mask"
    mod_idx = _mod_index(per_batch, max(seq_len // tm, 1))
    hb = tm // HALO
    n_halo_blocks = m // HALO
    nf = ff // tf
    return pl.pallas_call(
        functools.partial(_ffn_kernel, seq_len=seq_len, tm=tm),
        grid=(m // tm, nf),
        in_specs=[pl.BlockSpec((HALO, d), lambda i, f: (jnp.maximum(i * hb - 1, 0), 0)),
                  pl.BlockSpec((tm, d), lambda i, f: (i, 0)),
                  pl.BlockSpec((HALO, d), lambda i, f: (jnp.minimum((i + 1) * hb, n_halo_blocks - 1), 0)),
                  pl.BlockSpec((1, d), lambda i, f: (0, 0)),
                  pl.BlockSpec((1, 1, d), mod_idx),
                  pl.BlockSpec((1, 1, d), mod_idx),
                  pl.BlockSpec((1, 1, d), mod_idx),
                  pl.BlockSpec((None, d, tf), lambda i, f: (layer, 0, f)),
                  pl.BlockSpec((None, d, tf), lambda i, f: (layer, 0, nf + f)),
                  pl.BlockSpec((3, tf), lambda i, f: (0, f)),
                  pl.BlockSpec((1, tf), lambda i, f: (0, f)),
                  pl.BlockSpec((None, tf, d), lambda i, f: (layer, f, 0))],
        out_specs=pl.BlockSpec((tm, d), lambda i, f: (i, 0)),
        out_shape=jax.ShapeDtypeStruct((m, d), F32),
        scratch_shapes=[pltpu.VMEM((tm + 2 * HALO, d), BF16), pltpu.VMEM((tm, 1), F32)],
        compiler_params=_params("parallel", "arbitrary"),
        name="conv_glu",
    )(x, x, x, g, shift, scale, gate, w_up, w_up, conv_w, conv_b, w_down)


def _rope_tables(n, dim):
    rows = n // GRID_W
    row = jnp.repeat(jnp.arange(rows, dtype=F32), GRID_W)
    col = jnp.tile(jnp.arange(GRID_W, dtype=F32), rows)
    half = dim // 2
    inv = ROPE_BASE ** (-jnp.arange(0, half, 2, dtype=F32) / half)
    ar, ac = row[:, None] * inv, col[:, None] * inv
    cos = jnp.concatenate([jnp.cos(ar), jnp.cos(ar), jnp.cos(ac), jnp.cos(ac)], axis=-1)
    sin = jnp.concatenate([-jnp.sin(ar), jnp.sin(ar), -jnp.sin(ac), jnp.sin(ac)], axis=-1)
    reps = MXU_COLS // dim
    return jnp.tile(cos, (1, reps)), jnp.tile(sin, (1, reps))


def kernel(x_prompt, x_sample, cache_a_k, cache_a_v, cache_b_k, cache_b_v, c, c_ctx, ada_w, ada_b, norm1_g, norm2_g, a_w_qkv, a_q_norm, a_k_norm, a_sink, a_w_o, b_w_qkv, b_q_norm, b_k_norm, b_lambda_q1, b_lambda_k1, b_lambda_q2, b_lambda_k2, b_subln, b_w_o, ffn_w_up, ffn_conv_w, ffn_conv_b, ffn_w_down):
    batch, seq, d = x_prompt.shape
    dec_batch, dec_seq, _ = x_sample.shape
    depth = ada_w.shape[0]
    a_heads = a_sink.shape[1]
    a_kvh, a_hd = cache_a_k.shape[3], cache_a_k.shape[4]
    b_heads, b_qk = cache_b_k.shape[3], cache_b_k.shape[5]
    b_vd = cache_b_v.shape[4]
    past = cache_a_k.shape[2]
    assert a_hd == LANES and b_qk == LANES and b_vd == 2 * LANES

    n_cond = -(-(dec_batch + 1) // BF16_ROWS) * BF16_ROWS
    cond = jnp.concatenate([c, c_ctx[None, :], jnp.zeros((n_cond - dec_batch - 1, d), F32)], axis=0)
    mod = _modulation(cond, ada_w, ada_b).reshape(depth, n_cond, 6, d)
    cos, sin = _rope_tables(dec_seq, LANES)

    xp = x_prompt.reshape(batch * seq, d)
    xs = x_sample.reshape(dec_batch * dec_seq, d)
    n_a, n_b = (depth + 1) // 2, depth // 2
    a_k_state = a_v_state = b_k_state = b_v_state = None
    a_wqkv, a_wo, b_wqkv, b_wo, w_up, w_down = [w.astype(BF16) for w in
                                               (a_w_qkv, a_w_o, b_w_qkv, b_w_o, ffn_w_up, ffn_w_down)]

    for l in range(depth):
        mod_s = [mod[l, :dec_batch, t][:, None, :] for t in range(6)]
        mod_p = [mod[l, dec_batch:dec_batch + 1, t][:, None, :] for t in range(6)]
        n1 = norm1_g[l][None, :]
        n2 = norm2_g[l][None, :]
        j = l // 2
        if l % 2 == 0:
            w_qkv, w_o = a_wqkv, a_wo
            qn, kn = a_heads * a_hd, a_kvh * a_hd
            q_gain = a_q_norm[j] * (a_hd ** -0.5 * LOG2E)
            head_gain = jnp.concatenate([jnp.tile(q_gain, a_heads),
                                         jnp.tile(a_k_norm[j], a_kvh), jnp.ones((kn,), F32)])[None, :]
            sink_ok = (jnp.max(jnp.abs(a_sink[j])) * LOG2E <= SCORE_BOUND).astype(F32)
            sink_p = jnp.concatenate([a_sink[j], sink_ok * _scores_bounded(q_gain, a_k_norm[j], a_hd)[None]])[None, :]
            sink_s = jnp.concatenate([a_sink[j], sink_ok * _scores_bounded(q_gain, a_k_norm[j], a_hd,
                                                                          cache_a_k[:, j])[None]])[None, :]
            qkv_p, a_k_state, a_v_state = _qkv(
                xp, n1, mod_p[0], mod_p[1], w_qkv, j, head_gain, qn + kn, seq, None, BF16,
                tm=TILES.qkv_rows, tn=TILES.qkv_cols_context,
                state=_KvState(a_k_state, a_v_state, n_a, j, qn, kn, qn + kn, kn))
            op = _attn_a(qkv_p, sink_p, batch, seq, a_heads, a_kvh)
            qkv_s = _qkv(xs, n1, mod_s[0], mod_s[1], w_qkv, j, head_gain, qn + kn, dec_seq, (cos, sin), BF16,
                         tm=TILES.qkv_rows, tn=TILES.qkv_cols_latent)
            ctx_k = cache_a_k[:, j].reshape(dec_batch, past, kn).astype(BF16)
            ctx_v = cache_a_v[:, j].reshape(dec_batch, past, kn).astype(BF16)
            os_ = _attn_a(qkv_s, sink_s, dec_batch, dec_seq, a_heads, a_kvh, ctx_k, ctx_v)
        else:
            w_qkv, w_o = b_wqkv, b_wo
            lambda_init = 0.8 - 0.6 * math.exp(-0.3 * l)
            lam = (jnp.exp(jnp.sum(b_lambda_q1[j] * b_lambda_k1[j])) - jnp.exp(jnp.sum(b_lambda_q2[j] * b_lambda_k2[j]))
                   + lambda_init)
            qn = b_heads * 2 * b_qk
            q_gain = b_q_norm[j] * (b_qk ** -0.5 * LOG2E)
            head_gain = jnp.concatenate([jnp.tile(q_gain, 2 * b_heads),
                                         jnp.tile(b_k_norm[j], 2 * b_heads), jnp.ones((qn,), F32)])[None, :]
            subln = b_subln[j][None, :]
            scal_p = jnp.stack([lam, _scores_bounded(q_gain, b_k_norm[j], b_qk)])[None, :]
            scal_s = jnp.stack([lam, _scores_bounded(q_gain, b_k_norm[j], b_qk, cache_b_k[:, j])])[None, :]
            qkv_p, b_k_state, b_v_state = _qkv(
                xp, n1, mod_p[0], mod_p[1], w_qkv, j, head_gain, 2 * qn, seq, None, BF16,
                tm=TILES.qkv_rows, tn=TILES.qkv_cols_context,
                state=_KvState(b_k_state, b_v_state, n_b, j, qn, qn, 2 * qn, qn))
            op = _attn_b(qkv_p, scal_p, subln, 1.0 - lambda_init, batch, seq, b_heads)
            qkv_s = _qkv(xs, n1, mod_s[0], mod_s[1], w_qkv, j, head_gain, 2 * qn, dec_seq, (cos, sin), BF16,
                         tm=TILES.qkv_rows, tn=TILES.qkv_cols_latent)
            ctx_k = cache_b_k[:, j].reshape(dec_batch, past, qn).astype(BF16)
            ctx_v = cache_b_v[:, j].reshape(dec_batch, past, qn).astype(BF16)
            os_ = _attn_b(qkv_s, scal_s, subln, 1.0 - lambda_init, dec_batch, dec_seq, b_heads, ctx_k, ctx_v)
        xp = _proj_residual(op, w_o, j, xp, mod_p[2], seq, tm=TILES.proj_rows, tn=TILES.proj_cols)
        xs = _proj_residual(os_, w_o, j, xs, mod_s[2], dec_seq, tm=TILES.proj_rows, tn=TILES.proj_cols)
        cb = ffn_conv_b[l][None, :]
        xp = _ffn(xp, n2, mod_p[3], mod_p[4], mod_p[5], w_up, ffn_conv_w[l], cb, w_down, l, seq,
                  tm=TILES.ffn_rows, tf=TILES.ffn_cols)
        xs = _ffn(xs, n2, mod_s[3], mod_s[4], mod_s[5], w_up, ffn_conv_w[l], cb, w_down, l, dec_seq,
                  tm=TILES.ffn_rows, tf=TILES.ffn_cols)

    return (xp.reshape(batch, seq, d), xs.reshape(dec_batch, dec_seq, d),
            a_k_state.reshape(batch, n_a, seq, a_kvh, a_hd), a_v_state.reshape(batch, n_a, seq, a_kvh, a_hd),
            b_k_state.reshape(batch, n_b, seq, b_heads, 2, b_qk), b_v_state.reshape(batch, n_b, seq, b_heads, b_vd))
```

```python
import functools
import math
from typing import NamedTuple, Optional

import jax
import jax.numpy as jnp
from jax import lax
from jax.experimental import pallas as pl
from jax.experimental.pallas import tpu as pltpu

GRID_W = 64
A_WINDOW = 128
ROPE_BASE = 10000.0
EPS = 1e-6
NEG_INF = -1e30
LANES = 128
MXU_COLS = 256
BF16_ROWS = 16
LOG2E = math.log2(math.e)
VMEM_LIMIT = 56 * 1024 * 1024

BF16 = jnp.bfloat16
F32 = jnp.float32


class _Tiles(NamedTuple):
    modulation_cols: int = 1024
    qkv_rows: int = 1024
    qkv_cols_latent: int = 1536
    qkv_cols_context: int = 1024
    proj_rows: int = 512
    proj_cols: int = 2048
    ffn_rows: int = 1024
    ffn_cols: int = 512
    attn_a_queries: int = 256
    attn_b_queries: int = 1024
    attn_b_keys: int = 512


TILES = _Tiles()


def _params(*sem):
    return pltpu.CompilerParams(dimension_semantics=sem, vmem_limit_bytes=VMEM_LIMIT)


def _tile(size, target, *also):
    t = min(target, size)
    while any(s % t for s in (size,) + also):
        t //= 2
    return t


def _rms(x):
    return x * lax.rsqrt(jnp.mean(x * x, axis=-1, keepdims=True) + EPS)


def _adaln(x, g, shift, scale):
    return _rms(x) * g * (1.0 + scale) + shift


def _mod_index(per_batch, tiles_per_seq):
    if per_batch:
        return lambda i, j: (i // tiles_per_seq, 0, 0)
    return lambda i, j: (0, 0, 0)


def _modulation_kernel(c_ref, w_ref, b_ref, o_ref):
    c = c_ref[...]
    s = (c / (1.0 + jnp.exp(-c))).astype(BF16)
    o_ref[0] = jnp.dot(s, w_ref[0].astype(BF16), preferred_element_type=F32) + b_ref[0]


def _modulation(cond, ada_w, ada_b, tn=TILES.modulation_cols):
    depth, d, n = ada_w.shape
    r = cond.shape[0]
    return pl.pallas_call(
        _modulation_kernel,
        grid=(depth, n // tn),
        in_specs=[pl.BlockSpec((r, d), lambda l, j: (0, 0)),
                  pl.BlockSpec((1, d, tn), lambda l, j: (l, 0, j)),
                  pl.BlockSpec((1, 1, tn), lambda l, j: (l, 0, j))],
        out_specs=pl.BlockSpec((1, r, tn), lambda l, j: (l, 0, j)),
        out_shape=jax.ShapeDtypeStruct((depth, r, n), F32),
        compiler_params=_params("arbitrary", "arbitrary"),
        name="modulation",
    )(cond, ada_w, ada_b.reshape(depth, 1, n))


def _head_matrices():
    i = jnp.arange(MXU_COLS)
    mean = jnp.where(i[:, None] // LANES == i[None, :] // LANES, 1.0 / LANES, 0.0)
    quarter = LANES // 4
    partner = jnp.where(i % (2 * quarter) < quarter, i + quarter, i - quarter)
    perm = (i[:, None] == partner[None, :])
    return mean.astype(BF16), perm.astype(BF16)


class _KvState(NamedTuple):
    k_prev: Optional[jax.Array]
    v_prev: Optional[jax.Array]
    slots: int
    slot: int
    k_start: int
    k_width: int
    v_start: int
    v_width: int


def _qkv_kernel(*refs, n_norm_cols, n_tiles, rope, state, n_aliased):
    n_in = 7 + (3 if rope else 0) + n_aliased
    x_ref, g_ref, sh_ref, sc_ref, w_ref, hg_ref, mean_ref = refs[:7]
    if rope:
        perm_ref, cos_ref, sin_ref = refs[7:10]
    if state is not None:
        o_ref, ks_ref, vs_ref, h_ref, inv_ref = refs[n_in:]
    else:
        o_ref, h_ref, inv_ref = refs[n_in:]
    j = pl.program_id(1)

    @pl.when(j == 0)
    def _():
        inv_ref[...] = lax.rsqrt(jnp.mean(jnp.square(x_ref[...]), axis=-1, keepdims=True) + EPS)

    tm, tn = o_ref.shape
    subs = tn // MXU_COLS
    halves = 2 if tm % (2 * BF16_ROWS) == 0 and (state is None or ks_ref.shape[0] % 2 == 0) else 1
    hm = tm // halves

    def tile_body(tile):
        for r in range(halves):
            rows = slice(r * hm, (r + 1) * hm)
            if tile == 0:
                gain = g_ref[...] * (1.0 + sc_ref[0])
                h_ref[rows] = (x_ref[rows] * inv_ref[rows] * gain + sh_ref[0]).astype(BF16)
            y_all = jnp.dot(h_ref[rows], w_ref[...], preferred_element_type=F32)
            for c in range(subs):
                col = tile * tn + c * MXU_COLS
                sl = slice(c * MXU_COLS, (c + 1) * MXU_COLS)
                y = y_all[:, sl]
                if col < n_norm_cols:
                    ms = jnp.dot((y * y).astype(BF16), mean_ref[...], preferred_element_type=F32)
                    z = y * hg_ref[:, sl]
                    if rope:
                        partner = jnp.dot(z.astype(BF16), perm_ref[...], preferred_element_type=F32)
                        z = z * cos_ref[rows] + partner * sin_ref[rows]
                    y = z * lax.rsqrt(ms + EPS)
                o_ref[rows, sl] = y.astype(o_ref.dtype)
                if state is not None:
                    for ref, start, width in ((ks_ref, state.k_start, state.k_width),
                                              (vs_ref, state.v_start, state.v_width)):
                        if start <= col < start + width:
                            off = (col - start) % ref.shape[2]
                            sq = ref.shape[0] // halves
                            ref[r * sq:(r + 1) * sq, :, off:off + MXU_COLS] = y.reshape(sq, ref.shape[1], MXU_COLS)

    for tile in range(n_tiles):
        pl.when(j == tile)(functools.partial(tile_body, tile))


def _qkv(x, g, shift, scale, w, layer, head_gain, n_norm_cols, seq_len, rope_tabs, out_dtype, tm, tn, state=None):
    m, d = x.shape
    n = w.shape[2]
    per_batch = shift.shape[0] > 1
    tn = _tile(n, tn)
    assert tn % MXU_COLS == 0 and n_norm_cols % MXU_COLS == 0
    tm = _tile(m, tm, seq_len) if (per_batch or rope_tabs is not None) else _tile(m, tm)
    tiles_per_seq = max(seq_len // tm, 1)
    mod_idx = _mod_index(per_batch, tiles_per_seq)
    mean_mat, perm_mat = _head_matrices()
    const_spec = pl.BlockSpec((MXU_COLS, MXU_COLS), lambda i, j: (0, 0))
    in_specs = [pl.BlockSpec((tm, d), lambda i, j: (i, 0)),
                pl.BlockSpec((1, d), lambda i, j: (0, 0)),
                pl.BlockSpec((1, 1, d), mod_idx),
                pl.BlockSpec((1, 1, d), mod_idx),
                pl.BlockSpec((None, d, tn), lambda i, j: (layer, 0, j)),
                pl.BlockSpec((1, tn), lambda i, j: (0, j)),
                const_spec]
    args = [x, g, shift, scale, w, head_gain, mean_mat]
    if rope_tabs is not None:
        in_specs += [const_spec] + [pl.BlockSpec((tm, MXU_COLS), lambda i, j: (i % tiles_per_seq, 0))] * 2
        args += [perm_mat] + list(rope_tabs)
    out_specs = [pl.BlockSpec((tm, tn), lambda i, j: (i, j))]
    out_shape = [jax.ShapeDtypeStruct((m, n), out_dtype)]
    aliases = {}
    if state is not None:
        assert tm % seq_len == 0 and not per_batch
        seqs = tm // seq_len
        for start, width, prev in ((state.k_start, state.k_width, state.k_prev),
                                   (state.v_start, state.v_width, state.v_prev)):
            bw = min(tn, width)
            assert width % bw == 0 and start % MXU_COLS == 0 and (start % tn) % bw == 0
            idx = (lambda i, j, start=start, bw=bw, nblk=width // bw:
                   (i, state.slot, 0, jnp.clip((j * tn - start) // bw, 0, nblk - 1)))
            out_specs.append(pl.BlockSpec((seqs, None, seq_len, bw), idx))
            out_shape.append(jax.ShapeDtypeStruct((m // seq_len, state.slots, seq_len, width), F32))
            if prev is not None:
                aliases[len(args)] = len(out_shape) - 1
                in_specs.append(pl.BlockSpec(memory_space=pl.ANY))
                args.append(prev)
    out = pl.pallas_call(
        functools.partial(_qkv_kernel, n_norm_cols=n_norm_cols, n_tiles=n // tn, rope=rope_tabs is not None,
                          state=None if state is None else state._replace(k_prev=None, v_prev=None),
                          n_aliased=len(aliases)),
        grid=(m // tm, n // tn),
        in_specs=in_specs,
        out_specs=out_specs,
        out_shape=out_shape,
        input_output_aliases=aliases,
        scratch_shapes=[pltpu.VMEM((tm, d), BF16), pltpu.VMEM((tm, 1), F32)],
        compiler_params=_params("parallel", "arbitrary"),
        name="adaln_qkv",
    )(*args)
    return out if state is not None else out[0]


def _window_bias(tq, n_ctx):
    n_lat = tq + 2 * A_WINDOW
    qi = jnp.arange(tq)[:, None]
    kj = jnp.arange(n_lat + n_ctx)[None, :]
    band = jnp.abs(kj - A_WINDOW - qi) <= A_WINDOW
    variants = []
    for last in (False, True):
        for first in (False, True):
            in_seq = ((kj >= A_WINDOW) if first else True) & ((kj < tq + A_WINDOW) if last else True)
            variants.append(jnp.where((kj >= n_lat) | (band & in_seq), 0.0, NEG_INF))
    return jnp.stack(variants).astype(F32)


def _attn_a_kernel(*refs, groups, window, kv_per_step):
    if window:
        (sink_ref, bias_ref, q_ref, kp_ref, km_ref, kn_ref, kc_ref, vp_ref, vm_ref, vn_ref, vc_ref, o_ref) = refs
    else:
        sink_ref, q_ref, km_ref, vm_ref, o_ref = refs
    kvh0 = pl.program_id(2) * kv_per_step
    def head_rows(refs_, kk):
        cols = slice(kk * LANES, (kk + 1) * LANES)
        pieces = [r[0, :, cols] if len(r.shape) == 3 else r[:, cols] for r in refs_]
        return pieces[0].astype(BF16) if len(pieces) == 1 else jnp.concatenate(pieces, axis=0)

    k_refs = (kp_ref, km_ref, kn_ref, kc_ref) if window else (km_ref,)
    v_refs = (vp_ref, vm_ref, vn_ref, vc_ref) if window else (vm_ref,)
    keys = [head_rows(k_refs, kk) for kk in range(kv_per_step)]
    vals = [head_rows(v_refs, kk) for kk in range(kv_per_step)]
    n_heads = sink_ref.shape[1] - 1

    def heads(bounded):
        for kk in range(kv_per_step):
            for g in range(groups):
                sl = slice((kk * groups + g) * LANES, (kk * groups + g + 1) * LANES)
                q = q_ref[:, sl].astype(BF16)
                s = lax.dot_general(q, keys[kk], (((1,), (1,)), ((), ())), preferred_element_type=F32)
                if window:
                    s = s + bias_ref[...]
                sink = sink_ref[0, (kvh0 + kk) * groups + g] * LOG2E
                if bounded:
                    p = jnp.exp2(s)
                    denom = jnp.sum(p, axis=-1, keepdims=True) + jnp.exp2(jnp.full((1, 1), sink, F32))
                else:
                    m = jnp.maximum(jnp.max(s, axis=-1, keepdims=True), sink)
                    p = jnp.exp2(s - m)
                    denom = jnp.sum(p, axis=-1, keepdims=True) + jnp.exp2(sink - m)
                o = jnp.dot(p.astype(BF16), vals[kk], preferred_element_type=F32)
                o_ref[:, sl] = (o / denom).astype(o_ref.dtype)

    pl.when(sink_ref[0, n_heads] != 0.0)(functools.partial(heads, True))
    pl.when(sink_ref[0, n_heads] == 0.0)(functools.partial(heads, False))


def _attn_a(qkv, sink, batch, seq_len, heads, kv_heads, ctx_k=None, ctx_v=None, tq=TILES.attn_a_queries):
    m = qkv.shape[0]
    groups = heads // kv_heads
    window = ctx_k is not None
    tq = _tile(seq_len, tq)
    nb = seq_len // tq
    kps = kv_heads
    assert heads % kps == 0
    k_col, v_col = heads // kps, (heads + kv_heads) // kps
    r = tq // A_WINDOW
    rows128 = seq_len // A_WINDOW

    def main_rows(b, i):
        return b * nb + i

    def prev_rows(b, i):
        return b * rows128 + jnp.maximum(i * r - 1, 0)

    def next_rows(b, i):
        return b * rows128 + jnp.minimum((i + 1) * r, rows128 - 1)

    sink_spec = pl.BlockSpec(memory_space=pltpu.SMEM)
    q_spec = pl.BlockSpec((tq, kps * groups * LANES), lambda b, i, k: (main_rows(b, i), k))

    def kv_specs(col):
        main = pl.BlockSpec((tq, kps * LANES), lambda b, i, k: (main_rows(b, i), col + k))
        if not window:
            return [main]
        return [pl.BlockSpec((A_WINDOW, kps * LANES), lambda b, i, k: (prev_rows(b, i), col + k)),
                main,
                pl.BlockSpec((A_WINDOW, kps * LANES), lambda b, i, k: (next_rows(b, i), col + k)),
                pl.BlockSpec((1, ctx_k.shape[1], kps * LANES), lambda b, i, k: (b, 0, k))]

    lead_specs = [sink_spec]
    if window:
        bias = _window_bias(tq, ctx_k.shape[1])
        lead_specs.append(pl.BlockSpec((None,) + bias.shape[1:],
                                       lambda b, i, k: (jnp.where(i == 0, 1, 0) + jnp.where(i == nb - 1, 2, 0), 0, 0)))
        args = [sink, bias, qkv, qkv, qkv, qkv, ctx_k, qkv, qkv, qkv, ctx_v]
    else:
        args = [sink, qkv, qkv, qkv]
    return pl.pallas_call(
        functools.partial(_attn_a_kernel, groups=groups, window=window, kv_per_step=kps),
        grid=(batch, nb, kv_heads // kps),
        in_specs=lead_specs + [q_spec] + kv_specs(k_col) + kv_specs(v_col),
        out_specs=pl.BlockSpec((tq, kps * groups * LANES), lambda b, i, k: (main_rows(b, i), k)),
        out_shape=jax.ShapeDtypeStruct((m, heads * LANES), BF16),
        compiler_params=_params("parallel", "arbitrary", "arbitrary"),
        name="attn_a_window" if window else "attn_a_ctx",
    )(*args)


SCORE_BOUND = 40.0


def _scores_bounded(q_gain, k_gain, head_dim, ctx_k=None):
    q_norm = math.sqrt(head_dim) * jnp.max(jnp.abs(q_gain))
    k_norm = math.sqrt(head_dim) * jnp.max(jnp.abs(k_gain))
    if ctx_k is not None:
        k_norm = jnp.maximum(k_norm, jnp.sqrt(jnp.max(jnp.sum(jnp.square(ctx_k.astype(F32)), axis=-1))))
    return (q_norm * k_norm <= SCORE_BOUND).astype(F32)


def _lane_group_sum(p):
    return functools.reduce(jnp.add, [p[:, t * LANES:(t + 1) * LANES] for t in range(p.shape[1] // LANES)])


def _attn_b_kernel(*refs, has_ctx, out_scale, ck):
    if has_ctx:
        scal_ref, q_ref, k_ref, v_ref, kc_ref, vc_ref, sg_ref, o_ref = refs
    else:
        scal_ref, q_ref, k_ref, v_ref, sg_ref, o_ref = refs
    chunks =[(k_ref, v_ref, s0) for s0 in range(0, k_ref.shape[0], ck)]
    if has_ctx:
        chunks += [(kc_ref.at[0], vc_ref.at[0], s0) for s0 in range(0, kc_ref.shape[1], ck)]

    hd = 2 * LANES
    n_heads = o_ref.shape[1] // hd

    def scores(col, kr, s0):
        q = q_ref[:, col:col + LANES].astype(BF16)
        k = kr[s0:s0 + ck, col:col + LANES].astype(BF16)
        return lax.dot_general(q, k, (((1,), (1,)), ((), ())), preferred_element_type=F32)

    def finish(base, state):
        outs = [acc / jnp.sum(l, axis=-1, keepdims=True) for l, acc in state]
        o = outs[0] - scal_ref[0, 0] * outs[1]
        o_ref[:, base:base + hd] = (_rms(o) * sg_ref[...] * out_scale).astype(o_ref.dtype)

    def bounded_head(base):
        state = [None, None]
        for kr, vr, s0 in chunks:
            v = vr[s0:s0 + ck, base:base + hd].astype(BF16)
            for c in range(2):
                p = jnp.exp2(scores(base + c * LANES, kr, s0))
                l, acc = _lane_group_sum(p), jnp.dot(p.astype(BF16), v, preferred_element_type=F32)
                state[c] = (l, acc) if state[c] is None else (state[c][0] + l, state[c][1] + acc)
        finish(base, state)

    def online_head(base):
        state = [None, None]
        for kr, vr, s0 in chunks:
            v = vr[s0:s0 + ck, base:base + hd].astype(BF16)
            for c in range(2):
                s = scores(base + c * LANES, kr, s0)
                row_max = jnp.max(s, axis=-1, keepdims=True)
                if state[c] is None:
                    p = jnp.exp2(s - row_max)
                    state[c] = (row_max, _lane_group_sum(p), jnp.dot(p.astype(BF16), v, preferred_element_type=F32))
                else:
                    m_old, l_old, acc_old = state[c]
                    m_new = jnp.maximum(m_old, row_max)
                    alpha = jnp.exp2(m_old - m_new)
                    p = jnp.exp2(s - m_new)
                    state[c] = (m_new, alpha * l_old + _lane_group_sum(p),
                                alpha * acc_old + jnp.dot(p.astype(BF16), v, preferred_element_type=F32))
        finish(base, [(l, acc) for _, l, acc in state])

    @pl.when(scal_ref[0, 1] != 0.0)
    def _():
        for hh in range(n_heads):
            bounded_head(hh * hd)

    @pl.when(scal_ref[0, 1] == 0.0)
    def _():
        for hh in range(n_heads):
            online_head(hh * hd)


def _attn_b(qkv, scal, subln_g, out_scale, batch, seq_len, heads, ctx_k=None, ctx_v=None,
            tq=TILES.attn_b_queries, ck=TILES.attn_b_keys):
    m = qkv.shape[0]
    hd = 2 * LANES
    has_ctx = ctx_k is not None
    tq = _tile(seq_len, tq)
    ck = _tile(seq_len, ck, *([ctx_k.shape[1]] if has_ctx else []))
    nb = seq_len // tq
    hps = 1 if has_ctx else heads
    wd = hps * hd
    n_hblk = heads // hps
    in_specs = [pl.BlockSpec(memory_space=pltpu.SMEM),
                pl.BlockSpec((tq, wd), lambda b, h, i: (b * nb + i, h)),
                pl.BlockSpec((seq_len, wd), lambda b, h, i: (b, n_hblk + h)),
                pl.BlockSpec((seq_len, wd), lambda b, h, i: (b, 2 * n_hblk + h))]
    args = [scal, qkv, qkv, qkv]
    if has_ctx:
        in_specs += [pl.BlockSpec((1, ctx_k.shape[1], hd), lambda b, h, i: (b, 0, h))] * 2
        args += [ctx_k, ctx_v]
    in_specs.append(pl.BlockSpec((1, hd), lambda b, h, i: (0, 0)))
    args.append(subln_g)
    return pl.pallas_call(
        functools.partial(_attn_b_kernel, has_ctx=has_ctx, out_scale=out_scale, ck=ck),
        grid=(batch, n_hblk, nb),
        in_specs=in_specs,
        out_specs=pl.BlockSpec((tq, wd), lambda b, h, i: (b * nb + i, h)),
        out_shape=jax.ShapeDtypeStruct((m, heads * hd), BF16),
        compiler_params=_params("parallel", "arbitrary", "arbitrary"),
        name="attn_b_latent" if has_ctx else "attn_b_ctx",
    )(*args)


def _proj_residual_kernel(a_ref, w_ref, x_ref, gate_ref, o_ref):
    y = jnp.dot(a_ref[...], w_ref[...], preferred_element_type=F32)
    o_ref[...] = x_ref[...] + gate_ref[0] * y


def _proj_residual(a, w, layer, x, gate, seq_len, tm, tn):
    m, k = a.shape
    n = w.shape[2]
    per_batch = gate.shape[0] > 1
    tm = _tile(m, tm, seq_len) if per_batch else _tile(m, tm)
    tn = _tile(n, tn)
    mod_idx = _mod_index(per_batch, max(seq_len // tm, 1))
    gate_idx = lambda i, j: mod_idx(i, j)[:2] + (j,)
    return pl.pallas_call(
        _proj_residual_kernel,
        grid=(m // tm, n // tn),
        in_specs=[pl.BlockSpec((tm, k), lambda i, j: (i, 0)),
                  pl.BlockSpec((None, k, tn), lambda i, j: (layer, 0, j)),
                  pl.BlockSpec((tm, tn), lambda i, j: (i, j)),
                  pl.BlockSpec((1, 1, tn), gate_idx)],
        out_specs=pl.BlockSpec((tm, tn), lambda i, j: (i, j)),
        out_shape=jax.ShapeDtypeStruct((m, n), F32),
        compiler_params=_params("parallel", "arbitrary"),
        name="proj_residual",
    )(a, w, x, gate)


HALO = BF16_ROWS


def _ffn_kernel(xp_ref, x_ref, xn_ref, g_ref, sh_ref, sc_ref, gate_ref, wg_ref, wv_ref, cw_ref, cb_ref, wd_ref,
                o_ref, h_ref, inv_ref, *, seq_len, tm):
    i = pl.program_id(0)
    f = pl.program_id(1)

    @pl.when(f == 0)
    def _():
        inv_ref[...] = lax.rsqrt(jnp.mean(jnp.square(x_ref[...]), axis=-1, keepdims=True) + EPS)

    @pl.when(f == 0)
    def _():
        g, sh, sc = g_ref[...], sh_ref[0], sc_ref[0]
        h_ref[0:HALO] = _adaln(xp_ref[...], g, sh, sc).astype(BF16)
        h_ref[HALO:HALO + tm] = (x_ref[...] * inv_ref[...] * (g * (1.0 + sc)) + sh).astype(BF16)
        h_ref[HALO + tm:] = _adaln(xn_ref[...], g, sh, sc).astype(BF16)
        o_ref[...] = jnp.zeros_like(o_ref)

    gate_pre = jnp.dot(h_ref[...], wg_ref[...], preferred_element_type=F32)
    val = jnp.dot(h_ref[HALO:HALO + tm], wv_ref[...], preferred_element_type=F32)
    pos = (i * tm + lax.broadcasted_iota(jnp.int32, (tm, 1), 0)) & (seq_len - 1)
    up = jnp.where(pos != 0, gate_pre[HALO - 1:HALO - 1 + tm], 0.0)
    dn = jnp.where(pos != seq_len - 1, gate_pre[HALO + 1:HALO + 1 + tm], 0.0)
    gc = up * cw_ref[0:1] + gate_pre[HALO:HALO + tm] * cw_ref[1:2] + dn * cw_ref[2:3] + cb_ref[...]
    act = (gc / (1.0 + jnp.exp(-gc)) * val).astype(BF16)
    o_ref[...] += jnp.dot(act, wd_ref[...], preferred_element_type=F32)

    @pl.when(f == pl.num_programs(1) - 1)
    def _():
        o_ref[...] = x_ref[...] + gate_ref[0] * o_ref[...]


def _ffn(x, g, shift, scale, gate, w_up, conv_w, conv_b, w_down, layer, seq_len, tm, tf):
    m, d = x.shape
    ff = w_down.shape[1]
    per_batch = shift.shape[0] > 1
    tm = _tile(m, tm, seq_len) if per_batch else _tile(m, tm)
    tf = _tile(ff, tf)
    assert (seq_len % tm == 0 or tm % seq_len == 0) and tm % HALO == 0
    assert seq_len & (seq_len - 1) == 0, "token position uses a power-of-two mask"
    mod_idx = _mod_index(per_batch, max(seq_len // tm, 1))
    hb = tm // HALO
    n_halo_blocks = m // HALO
    nf = ff // tf
    return pl.pallas_call(
        functools.partial(_ffn_kernel, seq_len=seq_len, tm=tm),
        grid=(m // tm, nf),
        in_specs=[pl.BlockSpec((HALO, d), lambda i, f: (jnp.maximum(i * hb - 1, 0), 0)),
                  pl.BlockSpec((tm, d), lambda i, f: (i, 0)),
                  pl.BlockSpec((HALO, d), lambda i, f: (jnp.minimum((i + 1) * hb, n_halo_blocks - 1), 0)),
                  pl.BlockSpec((1, d), lambda i, f: (0, 0)),
                  pl.BlockSpec((1, 1, d), mod_idx),
                  pl.BlockSpec((1, 1, d), mod_idx),
                  pl.BlockSpec((1, 1, d), mod_idx),
                  pl.BlockSpec((None, d, tf), lambda i, f: (layer, 0, f)),
                  pl.BlockSpec((None, d, tf), lambda i, f: (layer, 0, nf + f)),
                  pl.BlockSpec((3, tf), lambda i, f: (0, f)),
                  pl.BlockSpec((1, tf), lambda i, f: (0, f)),
                  pl.BlockSpec((None, tf, d), lambda i, f: (layer, f, 0))],
        out_specs=pl.BlockSpec((tm, d), lambda i, f: (i, 0)),
        out_shape=jax.ShapeDtypeStruct((m, d), F32),
        scratch_shapes=[pltpu.VMEM((tm + 2 * HALO, d), BF16), pltpu.VMEM((tm, 1), F32)],
        compiler_params=_params("parallel", "arbitrary"),
        name="conv_glu",
    )(x, x, x, g, shift, scale, gate, w_up, w_up, conv_w, conv_b, w_down)


def _rope_tables(n, dim):
    rows = n // GRID_W
    row = jnp.repeat(jnp.arange(rows, dtype=F32), GRID_W)
    col = jnp.tile(jnp.arange(GRID_W, dtype=F32), rows)
    half = dim // 2
    inv = ROPE_BASE ** (-jnp.arange(0, half, 2, dtype=F32) / half)
    ar, ac = row[:, None] * inv, col[:, None] * inv
    cos = jnp.concatenate([jnp.cos(ar), jnp.cos(ar), jnp.cos(ac), jnp.cos(ac)], axis=-1)
    sin = jnp.concatenate([-jnp.sin(ar), jnp.sin(ar), -jnp.sin(ac), jnp.sin(ac)], axis=-1)
    reps = MXU_COLS // dim
    return jnp.tile(cos, (1, reps)), jnp.tile(sin, (1, reps))


def kernel(x_prompt, x_sample, cache_a_k, cache_a_v, cache_b_k, cache_b_v, c, c_ctx, ada_w, ada_b, norm1_g, norm2_g, a_w_qkv, a_q_norm, a_k_norm, a_sink, a_w_o, b_w_qkv, b_q_norm, b_k_norm, b_lambda_q1, b_lambda_k1, b_lambda_q2, b_lambda_k2, b_subln, b_w_o, ffn_w_up, ffn_conv_w, ffn_conv_b, ffn_w_down):
    batch, seq, d = x_prompt.shape
    dec_batch, dec_seq, _ = x_sample.shape
    depth = ada_w.shape[0]
    a_heads = a_sink.shape[1]
    a_kvh, a_hd = cache_a_k.shape[3], cache_a_k.shape[4]
    b_heads, b_qk = cache_b_k.shape[3], cache_b_k.shape[5]
    b_vd = cache_b_v.shape[4]
    past = cache_a_k.shape[2]
    assert a_hd == LANES and b_qk == LANES and b_vd == 2 * LANES

    n_cond = -(-(dec_batch + 1) // BF16_ROWS) * BF16_ROWS
    cond = jnp.concatenate([c, c_ctx[None, :], jnp.zeros((n_cond - dec_batch - 1, d), F32)], axis=0)
    mod = _modulation(cond, ada_w, ada_b).reshape(depth, n_cond, 6, d)
    cos, sin = _rope_tables(dec_seq, LANES)

    xp = x_prompt.reshape(batch * seq, d)
    xs = x_sample.reshape(dec_batch * dec_seq, d)
    n_a, n_b = (depth + 1) // 2, depth // 2
    a_k_state = a_v_state = b_k_state = b_v_state = None
    a_wqkv, a_wo, b_wqkv, b_wo, w_up, w_down = [w.astype(BF16) for w in
                                               (a_w_qkv, a_w_o, b_w_qkv, b_w_o, ffn_w_up, ffn_w_down)]

    for l in range(depth):
        mod_s = [mod[l, :dec_batch, t][:, None, :] for t in range(6)]
        mod_p = [mod[l, dec_batch:dec_batch + 1, t][:, None, :] for t in range(6)]
        n1 = norm1_g[l][None, :]
        n2 = norm2_g[l][None, :]
        j = l // 2
        if l % 2 == 0:
            w_qkv, w_o = a_wqkv, a_wo
            qn, kn = a_heads * a_hd, a_kvh * a_hd
            q_gain = a_q_norm[j] * (a_hd ** -0.5 * LOG2E)
            head_gain = jnp.concatenate([jnp.tile(q_gain, a_heads),
                                         jnp.tile(a_k_norm[j], a_kvh), jnp.ones((kn,), F32)])[None, :]
            sink_ok = (jnp.max(jnp.abs(a_sink[j])) * LOG2E <= SCORE_BOUND).astype(F32)
            sink_p = jnp.concatenate([a_sink[j], sink_ok * _scores_bounded(q_gain, a_k_norm[j], a_hd)[None]])[None, :]
            sink_s = jnp.concatenate([a_sink[j], sink_ok * _scores_bounded(q_gain, a_k_norm[j], a_hd,
                                                                          cache_a_k[:, j])[None]])[None, :]
            qkv_p, a_k_state, a_v_state = _qkv(
                xp, n1, mod_p[0], mod_p[1], w_qkv, j, head_gain, qn + kn, seq, None, BF16,
                tm=TILES.qkv_rows, tn=TILES.qkv_cols_context,
                state=_KvState(a_k_state, a_v_state, n_a, j, qn, kn, qn + kn, kn))
            op = _attn_a(qkv_p, sink_p, batch, seq, a_heads, a_kvh)
            qkv_s = _qkv(xs, n1, mod_s[0], mod_s[1], w_qkv, j, head_gain, qn + kn, dec_seq, (cos, sin), BF16,
                         tm=TILES.qkv_rows, tn=TILES.qkv_cols_latent)
            ctx_k = cache_a_k[:, j].reshape(dec_batch, past, kn).astype(BF16)
            ctx_v = cache_a_v[:, j].reshape(dec_batch, past, kn).astype(BF16)
            os_ = _attn_a(qkv_s, sink_s, dec_batch, dec_seq, a_heads, a_kvh, ctx_k, ctx_v)
        else:
            w_qkv, w_o = b_wqkv, b_wo
            lambda_init = 0.8 - 0.6 * math.exp(-0.3 * l)
            lam = (jnp.exp(jnp.sum(b_lambda_q1[j] * b_lambda_k1[j])) - jnp.exp(jnp.sum(b_lambda_q2[j] * b_lambda_k2[j]))
                   + lambda_init)
            qn = b_heads * 2 * b_qk
            q_gain = b_q_norm[j] * (b_qk ** -0.5 * LOG2E)
            head_gain = jnp.concatenate([jnp.tile(q_gain, 2 * b_heads),
                                         jnp.tile(b_k_norm[j], 2 * b_heads), jnp.ones((qn,), F32)])[None, :]
            subln = b_subln[j][None, :]
            scal_p = jnp.stack([lam, _scores_bounded(q_gain, b_k_norm[j], b_qk)])[None, :]
            scal_s = jnp.stack([lam, _scores_bounded(q_gain, b_k_norm[j], b_qk, cache_b_k[:, j])])[None, :]
            qkv_p, b_k_state, b_v_state = _qkv(
                xp, n1, mod_p[0], mod_p[1], w_qkv, j, head_gain, 2 * qn, seq, None, BF16,
                tm=TILES.qkv_rows, tn=TILES.qkv_cols_context,
                state=_KvState(b_k_state, b_v_state, n_b, j, qn, qn, 2 * qn, qn))
            op = _attn_b(qkv_p, scal_p, subln, 1.0 - lambda_init, batch, seq, b_heads)
            qkv_s = _qkv(xs, n1, mod_s[0], mod_s[1], w_qkv, j, head_gain, 2 * qn, dec_seq, (cos, sin), BF16,
                         tm=TILES.qkv_rows, tn=TILES.qkv_cols_latent)
            ctx_k = cache_b_k[:, j].reshape(dec_batch, past, qn).astype(BF16)
            ctx_v = cache_b_v[:, j].reshape(dec_batch, past, qn).astype(BF16)
            os_ = _attn_b(qkv_s, scal_s, subln, 1.0 - lambda_init, dec_batch, dec_seq, b_heads, ctx_k, ctx_v)
        xp = _proj_residual(op, w_o, j, xp, mod_p[2], seq, tm=TILES.proj_rows, tn=TILES.proj_cols)
        xs = _proj_residual(os_, w_o, j, xs, mod_s[2], dec_seq, tm=TILES.proj_rows, tn=TILES.proj_cols)
        cb = ffn_conv_b[l][None, :]
        xp = _ffn(xp, n2, mod_p[3], mod_p[4], mod_p[5], w_up, ffn_conv_w[l], cb, w_down, l, seq,
                  tm=TILES.ffn_rows, tf=TILES.ffn_cols)
        xs = _ffn(xs, n2, mod_s[3], mod_s[4], mod_s[5], w_up, ffn_conv_w[l], cb, w_down, l, dec_seq,
                  tm=TILES.ffn_rows, tf=TILES.ffn_cols)

    return (xp.reshape(batch, seq, d), xs.reshape(dec_batch, dec_seq, d),
            a_k_state.reshape(batch, n_a, seq, a_kvh, a_hd), a_v_state.reshape(batch, n_a, seq, a_kvh, a_hd),
            b_k_state.reshape(batch, n_b, seq, b_heads, 2, b_qk), b_v_state.reshape(batch, n_b, seq, b_heads, b_vd))
```
